```python
import math
import jax, jax.numpy as jnp
from jax import lax
import numpy as np

D_MODEL = 2048
BATCH = 2
SEQ = 8192
DEPTH = 4

CHUNK = 64

MIX_WIDTH = D_MODEL
POOL_WIDTH = MIX_WIDTH // 2
POOL_WINDOWS = (2, 4, 8, 16)
POOL_GROUPS = len(POOL_WINDOWS)
POOL_GROUP_DIM = POOL_WIDTH // POOL_GROUPS
ATT_WIDTH = MIX_WIDTH - POOL_WIDTH
ATT_HEAD_DIM = 64
ATT_HEADS = ATT_WIDTH // ATT_HEAD_DIM
LEFT_CHUNKS = 8
BAND = (LEFT_CHUNKS + 1) * CHUNK
REL_MAX = 128
REL_SIZE = CHUNK + REL_MAX
AB_IN_WIDTH = POOL_WIDTH + 3 * ATT_WIDTH

MLA_HEADS = 16
Q_LORA = 512
KV_LORA = 512
NOPE_DIM = 128
ROPE_DIM = 64
V_DIM = 128
ROPE_THETA = 10000.0
MLA_IN_WIDTH = Q_LORA + KV_LORA + ROPE_DIM
Q_BLOCK = 128

N_GROUPS = 8
EXPERTS_PER_GROUP = 8
N_EXPERTS = N_GROUPS * EXPERTS_PER_GROUP
EXPERT_HIDDEN = 512
TOP_K_IN_GROUP = 2
MOE_BLOCK = 128

DEEPNORM_ALPHA = (2 * DEPTH) ** 0.25
DEEPNORM_BETA = (8 * DEPTH) ** -0.25
LN_EPS = 1e-5
RMS_EPS = 1e-6

kernel_name = "hybrid_pool_band_mla_hmoe_deepnorm"


def layer_norm(x, g, b):
    xf = x.astype(jnp.float32)
    mu = jnp.mean(xf, axis=-1, keepdims=True)
    var = jnp.mean(jnp.square(xf - mu), axis=-1, keepdims=True)
    y = (xf - mu) * lax.rsqrt(var + LN_EPS) * g.astype(jnp.float32) + b.astype(jnp.float32)
    return y.astype(x.dtype)


def rms_norm(x, g):
    xf = x.astype(jnp.float32)
    y = xf * lax.rsqrt(jnp.mean(jnp.square(xf), axis=-1, keepdims=True) + RMS_EPS) * g.astype(jnp.float32)
    return y.astype(x.dtype)


def rope(x, cos, sin):
    half = ROPE_DIM // 2
    xf = x.astype(jnp.float32)
    x1, x2 = xf[..., :half], xf[..., half:]
    return jnp.concatenate([x1 * cos - x2 * sin, x1 * sin + x2 * cos], axis=-1).astype(x.dtype)


def multiscale_pool(u):
    B, S, _ = u.shape
    ug = u.reshape(B, S, POOL_GROUPS, POOL_GROUP_DIM).astype(jnp.float32)
    cs = jnp.cumsum(ug, axis=1)
    cs = jnp.concatenate([jnp.zeros_like(cs[:, :1]), cs], axis=1)
    win = jnp.array(POOL_WINDOWS, jnp.int32)
    t = jnp.arange(S, dtype=jnp.int32)[:, None]
    lo = jnp.maximum(t + 1 - win[None, :], 0)
    grp = jnp.arange(POOL_GROUPS, dtype=jnp.int32)[None, :]
    count = jnp.minimum(t + 1, win[None, :]).astype(jnp.float32)
    mean = (cs[:, 1:] - cs[:, lo, grp]) / count[None, :, :, None]
    return (mean - ug).astype(u.dtype)


def pool_mixer(u, w_pool, pool_scale):
    B, S, _ = u.shape
    d = multiscale_pool(u)
    y = jnp.einsum('bsgc,gcd->bsgd', d, w_pool)
    return y.reshape(B, S, POOL_WIDTH) * pool_scale


def chunk_band_attention(q, k, v, rel_bias):
    B, S, H, Dh = q.shape
    n_chunks = S // CHUNK
    pad = LEFT_CHUNKS * CHUNK
    k_pad = jnp.pad(k, ((0, 0), (pad, 0), (0, 0), (0, 0)))
    v_pad = jnp.pad(v, ((0, 0), (pad, 0), (0, 0), (0, 0)))
    qc = jnp.swapaxes(q.reshape(B, n_chunks, CHUNK, H, Dh), 0, 1)
    qi = jnp.arange(CHUNK, dtype=jnp.int32)[:, None]
    kj = jnp.arange(BAND, dtype=jnp.int32)[None, :]
    dist = pad + qi - kj
    bias = rel_bias[:, jnp.clip(dist, -(CHUNK - 1), REL_MAX) + CHUNK - 1].astype(jnp.float32)
    scale = Dh ** -0.5

    def one_chunk(args):
        c, qb = args
        start = c * CHUNK
        kb = lax.dynamic_slice_in_dim(k_pad, start, BAND, axis=1)
        vb = lax.dynamic_slice_in_dim(v_pad, start, BAND, axis=1)
        s = jnp.einsum('bqhd,bkhd->bhqk', qb, kb).astype(jnp.float32) * scale + bias
        valid = (start - pad + jnp.arange(BAND, dtype=jnp.int32)) >= 0
        s = jnp.where(valid[None, None, None, :], s, -jnp.inf)
        p = jax.nn.softmax(s, axis=-1).astype(vb.dtype)
        return jnp.einsum('bhqk,bkhd->bqhd', p, vb)

    out = lax.map(one_chunk, (jnp.arange(n_chunks, dtype=jnp.int32), qc))
    return jnp.swapaxes(out, 0, 1).reshape(B, S, H * Dh)


def pool_band_mixer(x, w_in, w_pool, pool_scale, rel_bias, w_o):
    B, S, _ = x.shape
    h = x @ w_in
    u = h[..., :POOL_WIDTH]
    q = h[..., POOL_WIDTH:POOL_WIDTH + ATT_WIDTH].reshape(B, S, ATT_HEADS, ATT_HEAD_DIM)
    k = h[..., POOL_WIDTH + ATT_WIDTH:POOL_WIDTH + 2 * ATT_WIDTH].reshape(B, S, ATT_HEADS, ATT_HEAD_DIM)
    v = h[..., POOL_WIDTH + 2 * ATT_WIDTH:].reshape(B, S, ATT_HEADS, ATT_HEAD_DIM)
    a_out = pool_mixer(u, w_pool, pool_scale)
    b_out = chunk_band_attention(q, k, v, rel_bias)
    return jnp.concatenate([a_out, b_out], axis=-1) @ w_o


def block_causal_latent_attention(q_nope, q_rope, k_nope, k_rope, v):
    B, S, H, _ = q_nope.shape
    n_blocks = S // Q_BLOCK
    scale = (NOPE_DIM + ROPE_DIM) ** -0.5
    key_chunk = jnp.arange(S, dtype=jnp.int32) // CHUNK
    qn = jnp.swapaxes(q_nope.reshape(B, n_blocks, Q_BLOCK, H, NOPE_DIM), 0, 1)
    qr = jnp.swapaxes(q_rope.reshape(B, n_blocks, Q_BLOCK, H, ROPE_DIM), 0, 1)

    def one_block(args):
        i, qnb, qrb = args
        s = (jnp.einsum('bqhd,bkhd->bhqk', qnb, k_nope)
             + jnp.einsum('bqhr,bkr->bhqk', qrb, k_rope)).astype(jnp.float32) * scale
        q_chunk = (i * Q_BLOCK + jnp.arange(Q_BLOCK, dtype=jnp.int32)) // CHUNK
        mask = key_chunk[None, :] <= q_chunk[:, None]
        s = jnp.where(mask[None, None], s, -jnp.inf)
        p = jax.nn.softmax(s, axis=-1).astype(v.dtype)
        return jnp.einsum('bhqk,bkhd->bqhd', p, v)

    out = lax.map(one_block, (jnp.arange(n_blocks, dtype=jnp.int32), qn, qr))
    return jnp.swapaxes(out, 0, 1).reshape(B, S, H * V_DIM)


def mla_mixer(x, w_in, g_q, g_kv, w_uq, w_uk, w_uv, w_o):
    B, S, _ = x.shape
    h = x @ w_in
    cq = rms_norm(h[..., :Q_LORA], g_q)
    ckv = rms_norm(h[..., Q_LORA:Q_LORA + KV_LORA], g_kv)
    kr = h[..., Q_LORA + KV_LORA:]
    half = ROPE_DIM // 2
    freq = ROPE_THETA ** (-jnp.arange(half, dtype=jnp.float32) / half)
    ang = jnp.arange(S, dtype=jnp.float32)[:, None] * freq[None, :]
    cos, sin = jnp.cos(ang), jnp.sin(ang)
    q = (cq @ w_uq).reshape(B, S, MLA_HEADS, NOPE_DIM + ROPE_DIM)
    q_nope = q[..., :NOPE_DIM]
    q_rope = rope(q[..., NOPE_DIM:], cos[:, None, :], sin[:, None, :])
    k_rope = rope(kr, cos, sin)
    k_nope = (ckv @ w_uk).reshape(B, S, MLA_HEADS, NOPE_DIM)
    v = (ckv @ w_uv).reshape(B, S, MLA_HEADS, V_DIM)
    o = block_causal_latent_attention(q_nope, q_rope, k_nope, k_rope, v)
    return o @ w_o


def hier_moe(x2d, w_group, b_group, w_expert, b_expert, w_gate, w_up, w_down):
    N, D = x2d.shape
    dt = x2d.dtype
    lg = (x2d @ w_group).astype(jnp.float32) + b_group.astype(jnp.float32)
    p_group = jax.nn.softmax(lg, axis=-1)
    g_sel = jnp.argmax(lg, axis=-1).astype(jnp.int32)
    pg = jnp.take_along_axis(p_group, g_sel[:, None], axis=1)
    le = ((x2d @ w_expert).astype(jnp.float32) + b_expert.astype(jnp.float32)).reshape(N, N_GROUPS, EXPERTS_PER_GROUP)
    le_sel = jnp.take_along_axis(le, g_sel[:, None, None], axis=1)[:, 0]
    top_v, top_i = lax.top_k(le_sel, TOP_K_IN_GROUP)
    gate = pg * jax.nn.softmax(top_v, axis=-1)
    e_flat = (g_sel[:, None] * EXPERTS_PER_GROUP + top_i.astype(jnp.int32)).reshape(-1)
    tok_flat = jnp.repeat(jnp.arange(N, dtype=jnp.int32), TOP_K_IN_GROUP)
    g_flat = gate.reshape(-1)
    nk = N * TOP_K_IN_GROUP
    order = jnp.argsort(e_flat)
    se, stok, sg = e_flat[order], tok_flat[order], g_flat[order]
    counts = jnp.bincount(e_flat, length=N_EXPERTS)
    starts = jnp.cumsum(counts) - counts
    padded = ((counts + MOE_BLOCK - 1) // MOE_BLOCK) * MOE_BLOCK
    pends = jnp.cumsum(padded)
    pstarts = pends - padded
    dest = pstarts[se] + (jnp.arange(nk, dtype=jnp.int32) - starts[se])
    n_blocks = -(-nk // MOE_BLOCK) + N_EXPERTS
    buf_tok = jnp.zeros((n_blocks * MOE_BLOCK,), jnp.int32).at[dest].set(stok)
    buf_gate = jnp.zeros((n_blocks * MOE_BLOCK,), dt).at[dest].set(sg.astype(dt))
    block_start = jnp.arange(n_blocks, dtype=jnp.int32) * MOE_BLOCK
    block_e = jnp.minimum(jnp.searchsorted(pends, block_start, side='right'), N_EXPERTS - 1)

    def run_block(args):
        tok_b, gate_b, e = args
        xb = x2d[tok_b]
        hb = jax.nn.silu(xb @ w_gate[e]) * (xb @ w_up[e])
        return (hb @ w_down[e]) * gate_b[:, None]

    y = lax.map(run_block, (buf_tok.reshape(n_blocks, MOE_BLOCK), buf_gate.reshape(n_blocks, MOE_BLOCK), block_e))
    return jnp.zeros_like(x2d).at[buf_tok].add(y.reshape(-1, D))


def setup_inputs(seed: int = 0) -> dict:
    key = jax.random.key(seed)
    ks = jax.random.split(key, 24)
    f32 = jnp.float32
    L2 = DEPTH // 2
    beta = DEEPNORM_BETA

    def nrm(k, shape, fan_in, scale=1.0):
        return jax.random.normal(k, shape, f32) * (scale * fan_in ** -0.5)

    def gain(k, shape):
        return 1.0 + 0.02 * jax.random.normal(k, shape, f32)

    x = jax.random.normal(ks[0], (BATCH, SEQ, D_MODEL), f32)
    v_cols = jnp.concatenate([jnp.ones((POOL_WIDTH + 2 * ATT_WIDTH,), f32), jnp.full((ATT_WIDTH,), beta, f32)])
    ab_w_in = nrm(ks[1], (L2, D_MODEL, AB_IN_WIDTH), D_MODEL) * v_cols
    ab_w_pool = nrm(ks[2], (L2, POOL_GROUPS, POOL_GROUP_DIM, POOL_GROUP_DIM), POOL_GROUP_DIM)
    ab_pool_scale = gain(ks[3], (L2, POOL_WIDTH))
    ab_rel_bias = 0.5 * jax.random.normal(ks[4], (L2, ATT_HEADS, REL_SIZE), f32)
    ab_w_o = nrm(ks[5], (L2, MIX_WIDTH, D_MODEL), MIX_WIDTH, beta)
    mla_w_in = nrm(ks[6], (L2, D_MODEL, MLA_IN_WIDTH), D_MODEL)
    mla_g_q = gain(ks[7], (L2, Q_LORA))
    mla_g_kv = gain(ks[8], (L2, KV_LORA))
    mla_w_uq = nrm(ks[9], (L2, Q_LORA, MLA_HEADS * (NOPE_DIM + ROPE_DIM)), Q_LORA)
    mla_w_uk = nrm(ks[10], (L2, KV_LORA, MLA_HEADS * NOPE_DIM), KV_LORA)
    mla_w_uv = nrm(ks[11], (L2, KV_LORA, MLA_HEADS * V_DIM), KV_LORA, beta)
    mla_w_o = nrm(ks[12], (L2, MLA_HEADS * V_DIM, D_MODEL), MLA_HEADS * V_DIM, beta)
    ln1_g = gain(ks[13], (DEPTH, D_MODEL))
    ln1_b = 0.02 * jax.random.normal(ks[14], (DEPTH, D_MODEL), f32)
    moe_w_group = nrm(ks[15], (DEPTH, D_MODEL, N_GROUPS), D_MODEL)
    moe_b_group = 0.01 * jax.random.normal(ks[16], (DEPTH, N_GROUPS), f32)
    moe_w_expert = nrm(ks[17], (DEPTH, D_MODEL, N_EXPERTS), D_MODEL)
    moe_b_expert = 0.01 * jax.random.normal(ks[18], (DEPTH, N_EXPERTS), f32)
    moe_w_gate = nrm(ks[19], (DEPTH, N_EXPERTS, D_MODEL, EXPERT_HIDDEN), D_MODEL, beta)
    moe_w_up = nrm(ks[20], (DEPTH, N_EXPERTS, D_MODEL, EXPERT_HIDDEN), D_MODEL, beta)
    moe_w_down = nrm(ks[21], (DEPTH, N_EXPERTS, EXPERT_HIDDEN, D_MODEL), EXPERT_HIDDEN, beta)
    ln2_g = gain(ks[22], (DEPTH, D_MODEL))
    ln2_b = 0.02 * jax.random.normal(ks[23], (DEPTH, D_MODEL), f32)
    return {"x": x, "ab_w_in": ab_w_in, "ab_w_pool": ab_w_pool, "ab_pool_scale": ab_pool_scale,
            "ab_rel_bias": ab_rel_bias, "ab_w_o": ab_w_o, "mla_w_in": mla_w_in, "mla_g_q": mla_g_q,
            "mla_g_kv": mla_g_kv, "mla_w_uq": mla_w_uq, "mla_w_uk": mla_w_uk, "mla_w_uv": mla_w_uv,
            "mla_w_o": mla_w_o, "ln1_g": ln1_g, "ln1_b": ln1_b, "moe_w_group": moe_w_group,
            "moe_b_group": moe_b_group, "moe_w_expert": moe_w_expert, "moe_b_expert": moe_b_expert,
            "moe_w_gate": moe_w_gate, "moe_w_up": moe_w_up, "moe_w_down": moe_w_down,
            "ln2_g": ln2_g, "ln2_b": ln2_b}


def reference(x, ab_w_in, ab_w_pool, ab_pool_scale, ab_rel_bias, ab_w_o, mla_w_in, mla_g_q, mla_g_kv,
              mla_w_uq, mla_w_uk, mla_w_uv, mla_w_o, ln1_g, ln1_b, moe_w_group, moe_b_group,
              moe_w_expert, moe_b_expert, moe_w_gate, moe_w_up, moe_w_down, ln2_g, ln2_b):
    B, S, D = x.shape
    for i in range(DEPTH):
        j = i // 2
        if i % 2 == 0:
            m = pool_band_mixer(x, ab_w_in[j], ab_w_pool[j], ab_pool_scale[j], ab_rel_bias[j], ab_w_o[j])
        else:
            m = mla_mixer(x, mla_w_in[j], mla_g_q[j], mla_g_kv[j], mla_w_uq[j], mla_w_uk[j], mla_w_uv[j], mla_w_o[j])
        x = layer_norm(DEEPNORM_ALPHA * x + m, ln1_g[i], ln1_b[i])
        f = hier_moe(x.reshape(B * S, D), moe_w_group[i], moe_b_group[i], moe_w_expert[i], moe_b_expert[i],
                     moe_w_gate[i], moe_w_up[i], moe_w_down[i]).reshape(B, S, D)
        x = layer_norm(DEEPNORM_ALPHA * x + f, ln2_g[i], ln2_b[i])
    return x
```

```python
import functools

import jax
import jax.numpy as jnp
from jax import lax
from jax.experimental import pallas as pl
from jax.experimental.pallas import tpu as pltpu

F32 = jnp.float32
BF16 = jnp.bfloat16
I32 = jnp.int32

D_MODEL = 2048
DEPTH = 4
CHUNK = 64
CHUNK_SHIFT = 6
GROUP_SHIFT = 3
POOL_WINDOWS = (2, 4, 8, 16)
POOL_GROUP_DIM = 256
POOL_WIDTH = 1024
ATT_WIDTH = 1024
ATT_HEADS = 16
ATT_HEAD_DIM = 64
LEFT_CHUNKS = 8
REL_MAX = 128
MLA_HEADS = 16
Q_LORA = 512
KV_LORA = 512
NOPE_DIM = 128
ROPE_DIM = 64
V_DIM = 128
ROPE_THETA = 10000.0
N_GROUPS = 8
EXPERTS_PER_GROUP = 8
N_EXPERTS = 64
EXPERT_HIDDEN = 512
DEEPNORM_ALPHA = (2 * DEPTH) ** 0.25
LN_EPS = 1e-5
RMS_EPS = 1e-6

LANES = 128
POOL_HALO = 16
BAND_SUB = 128
BAND_KEYS = BAND_SUB + LEFT_CHUNKS * CHUNK
NEG_INF = float("-inf")
VMEM_LIMIT = 56 * 1024 * 1024


def _cparams(sem, vmem=VMEM_LIMIT):
    return pltpu.CompilerParams(dimension_semantics=sem, vmem_limit_bytes=vmem)


def _mm_kernel(x_ref, w_ref, o_ref):
    o_ref[...] = jnp.dot(x_ref[...].astype(BF16), w_ref[...], preferred_element_type=F32).astype(o_ref.dtype)


def matmul(x, w, *, tm, tn, out_dtype):
    m, k = x.shape
    n = w.shape[1]
    return pl.pallas_call(
        _mm_kernel,
        grid=(m // tm, n // tn),
        in_specs=[pl.BlockSpec((tm, k), lambda i, j: (i, 0)), pl.BlockSpec((k, tn), lambda i, j: (0, j))],
        out_specs=pl.BlockSpec((tm, tn), lambda i, j: (i, j)),
        out_shape=jax.ShapeDtypeStruct((m, n), out_dtype),
        compiler_params=_cparams(("parallel", "parallel")),
        name="matmul",
    )(x, w)


def _pool_kernel(u_ref, halo_ref, w_ref, scale_ref, o_ref, *, tm, seq):
    i = pl.program_id(0)
    pos0 = (i * tm) % seq
    u = u_ref[...].astype(F32)
    halo = jnp.where(pos0 == 0, 0.0, halo_ref[...].astype(F32))
    ue = jnp.concatenate([halo, u], axis=0)
    pos = pos0 + lax.broadcasted_iota(I32, (tm, 1), 0)
    for g, win in enumerate(POOL_WINDOWS):
        cols = slice(g * POOL_GROUP_DIM, (g + 1) * POOL_GROUP_DIM)
        s = ue[:, cols]
        shift = 1
        while shift < win:
            s = s + pltpu.roll(s, shift, axis=0)
            shift *= 2
        wsum = s[POOL_HALO:, :]
        count = jnp.minimum(pos + 1, win).astype(F32)
        d = wsum / count - u[:, cols]
        y = jnp.dot(d.astype(BF16), w_ref[g], preferred_element_type=F32) * scale_ref[:, cols]
        o_ref[:, cols] = y.astype(o_ref.dtype)


def pool_mixer(h, w_pool, pool_scale, *, seq, tm):
    n = h.shape[0]
    hb = tm // POOL_HALO
    return pl.pallas_call(
        functools.partial(_pool_kernel, tm=tm, seq=seq),
        grid=(n // tm,),
        in_specs=[
            pl.BlockSpec((tm, POOL_WIDTH), lambda i: (i, 0)),
            pl.BlockSpec((POOL_HALO, POOL_WIDTH), lambda i: (jnp.maximum(i * hb - 1, 0), 0)),
            pl.BlockSpec((len(POOL_WINDOWS), POOL_GROUP_DIM, POOL_GROUP_DIM), lambda i: (0, 0, 0)),
            pl.BlockSpec((1, POOL_WIDTH), lambda i: (0, 0)),
        ],
        out_specs=pl.BlockSpec((tm, POOL_WIDTH), lambda i: (i, 0)),
        out_shape=jax.ShapeDtypeStruct((n, POOL_WIDTH), BF16),
        compiler_params=_cparams(("parallel",)),
        name="pool_mixer",
    )(h, h, w_pool, pool_scale)


def _band_kernel(q_ref, kp_ref, kc_ref, vp_ref, vc_ref, bias_ref, o_ref, *, tq):
    i = pl.program_id(1)
    kk = jnp.concatenate([kp_ref[...], kc_ref[...]], axis=0)
    vv = jnp.concatenate([vp_ref[...], vc_ref[...]], axis=0)
    lane = lax.broadcasted_iota(I32, (1, LANES), 1)
    first_head = lane < ATT_HEAD_DIM
    col = lax.broadcasted_iota(I32, (1, BAND_KEYS), 1)
    left = LEFT_CHUNKS * CHUNK
    for j in range(tq // BAND_SUB):
        qs = q_ref[j * BAND_SUB:(j + 1) * BAND_SUB, :]
        off = tq - left + j * BAND_SUB
        kw = kk[off:off + BAND_KEYS, :]
        vw = vv[off:off + BAND_KEYS, :]
        valid = (i * tq + j * BAND_SUB - left + col) >= 0
        outs = []
        for hh in range(2):
            qm = jnp.where(first_head if hh == 0 else jnp.logical_not(first_head), qs, jnp.zeros_like(qs))
            s = lax.dot_general(qm, kw, (((1,), (1,)), ((), ())), preferred_element_type=F32)
            s = jnp.where(valid, s + bias_ref[hh], NEG_INF)
            m = jnp.max(s, axis=1, keepdims=True)
            p = jnp.exp(s - m)
            l = jnp.sum(p, axis=1, keepdims=True)
            o = jnp.dot(p.astype(BF16), vw, preferred_element_type=F32)
            outs.append(o / l)
        o_ref[j * BAND_SUB:(j + 1) * BAND_SUB, :] = jnp.where(first_head, outs[0], outs[1]).astype(o_ref.dtype)


def band_bias_table(rel_bias):
    r = jnp.arange(BAND_SUB, dtype=I32)[:, None]
    c = jnp.arange(BAND_KEYS, dtype=I32)[None, :]
    dist = LEFT_CHUNKS * CHUNK + r - c
    qc, kc = r // CHUNK, c // CHUNK
    allowed = (kc >= qc) & (kc <= qc + LEFT_CHUNKS)
    idx = jnp.clip(dist, -(CHUNK - 1), REL_MAX) + CHUNK - 1
    return jnp.where(allowed[None], rel_bias[:, idx].astype(F32), NEG_INF)


def band_attention(h, bias_tab, *, batch, seq, tq):
    n = h.shape[0]
    nq = seq // tq
    qcol = POOL_WIDTH // LANES
    kcol = qcol + ATT_WIDTH // LANES
    vcol = kcol + ATT_WIDTH // LANES
    cur = lambda c0: (lambda b, i, p: (b * nq + i, c0 + p))
    prev = lambda c0: (lambda b, i, p: (b * nq + jnp.maximum(i - 1, 0), c0 + p))
    blk = (tq, LANES)
    return pl.pallas_call(
        functools.partial(_band_kernel, tq=tq),
        grid=(batch, nq, ATT_HEADS // 2),
        in_specs=[
            pl.BlockSpec(blk, cur(qcol)),
            pl.BlockSpec(blk, prev(kcol)), pl.BlockSpec(blk, cur(kcol)),
            pl.BlockSpec(blk, prev(vcol)), pl.BlockSpec(blk, cur(vcol)),
            pl.BlockSpec((2, BAND_SUB, BAND_KEYS), lambda b, i, p: (p, 0, 0)),
        ],
        out_specs=pl.BlockSpec(blk, lambda b, i, p: (b * nq + i, p)),
        out_shape=jax.ShapeDtypeStruct((n, ATT_WIDTH), BF16),
        compiler_params=_cparams(("parallel", "parallel", "parallel")),
        name="band_attention",
    )(h, h, h, h, h, bias_tab)


def _layer_norm_rows(z, g, b):
    mu = jnp.mean(z, axis=-1, keepdims=True)
    zc = z - mu
    var = jnp.mean(zc * zc, axis=-1, keepdims=True)
    return zc * lax.rsqrt(var + LN_EPS) * g + b


def _proj_ln_kernel(*refs, n_in):
    a_refs, w_refs = refs[:n_in], refs[n_in:2 * n_in]
    x_ref, g_ref, b_ref, o_ref = refs[2 * n_in:]
    y = jnp.dot(a_refs[0][...], w_refs[0][...], preferred_element_type=F32)
    for a_ref, w_ref in zip(a_refs[1:], w_refs[1:]):
        y = y + jnp.dot(a_ref[...], w_ref[...], preferred_element_type=F32)
    z = DEEPNORM_ALPHA * x_ref[...] + y
    o_ref[...] = _layer_norm_rows(z, g_ref[...], b_ref[...])


def proj_residual_ln(acts, weights, x, g, b, *, tm):
    n, d = x.shape
    n_in = len(acts)
    in_specs = [pl.BlockSpec((tm, a.shape[1]), lambda i: (i, 0)) for a in acts]
    in_specs += [pl.BlockSpec(w.shape, lambda i: (0, 0)) for w in weights]
    in_specs += [pl.BlockSpec((tm, d), lambda i: (i, 0)), pl.BlockSpec((1, d), lambda i: (0, 0)),
                 pl.BlockSpec((1, d), lambda i: (0, 0))]
    return pl.pallas_call(
        functools.partial(_proj_ln_kernel, n_in=n_in),
        grid=(n // tm,),
        in_specs=in_specs,
        out_specs=pl.BlockSpec((tm, d), lambda i: (i, 0)),
        out_shape=jax.ShapeDtypeStruct((n, d), F32),
        compiler_params=_cparams(("parallel",)),
        name="proj_residual_ln",
    )(*acts, *weights, x, g.reshape(1, d), b.reshape(1, d))


def _rope_lanes(x, c_ref, s_up_ref, s_dn_ref):
    half = ROPE_DIM // 2
    return (x * c_ref[...] + pltpu.roll(x, half, axis=1) * s_up_ref[...]
            + pltpu.roll(x, LANES - half, axis=1) * s_dn_ref[...])


def _rms_rows(x, g):
    return x * lax.rsqrt(jnp.mean(x * x, axis=-1, keepdims=True) + RMS_EPS) * g


def _mla_in_kernel(x_ref, w_ref, gq_ref, gkv_ref, c_ref, su_ref, sd_ref, cq_ref, ckv_ref, kr_ref):
    h = jnp.dot(x_ref[...].astype(BF16), w_ref[...], preferred_element_type=F32)
    cq_ref[...] = _rms_rows(h[:, :Q_LORA], gq_ref[...]).astype(cq_ref.dtype)
    ckv_ref[...] = _rms_rows(h[:, Q_LORA:Q_LORA + KV_LORA], gkv_ref[...]).astype(ckv_ref.dtype)
    kr_ref[...] = _rope_lanes(h[:, Q_LORA + KV_LORA:], c_ref, su_ref, sd_ref).astype(kr_ref.dtype)


def mla_in_proj(x, w_in_pad, g_q, g_kv, rope_tabs, *, seq, tm):
    n, d = x.shape
    wn = w_in_pad.shape[1]
    npos = seq // tm
    tab = pl.BlockSpec((tm, LANES), lambda i: (i % npos, 0))
    return pl.pallas_call(
        _mla_in_kernel,
        grid=(n // tm,),
        in_specs=[pl.BlockSpec((tm, d), lambda i: (i, 0)), pl.BlockSpec((d, wn), lambda i: (0, 0)),
                  pl.BlockSpec((1, Q_LORA), lambda i: (0, 0)), pl.BlockSpec((1, KV_LORA), lambda i: (0, 0)),
                  tab, tab, tab],
        out_specs=[pl.BlockSpec((tm, Q_LORA), lambda i: (i, 0)), pl.BlockSpec((tm, KV_LORA), lambda i: (i, 0)),
                   pl.BlockSpec((tm, LANES), lambda i: (i, 0))],
        out_shape=[jax.ShapeDtypeStruct((n, Q_LORA), BF16), jax.ShapeDtypeStruct((n, KV_LORA), BF16),
                   jax.ShapeDtypeStruct((n, LANES), BF16)],
        compiler_params=_cparams(("parallel",)),
        name="mla_in_proj",
    )(x, w_in_pad, g_q.reshape(1, -1), g_kv.reshape(1, -1), *rope_tabs)


def _mla_q_kernel(cq_ref, w_ref, c_ref, su_ref, sd_ref, o_ref, *, heads, scale):
    q = jnp.dot(cq_ref[...], w_ref[...], preferred_element_type=F32)
    for hh in range(heads):
        base = hh * 2 * LANES
        o_ref[:, base:base + LANES] = (q[:, base:base + LANES] * scale).astype(o_ref.dtype)
        rot = _rope_lanes(q[:, base + LANES:base + 2 * LANES], c_ref, su_ref, sd_ref)
        o_ref[:, base + LANES:base + 2 * LANES] = (rot * scale).astype(o_ref.dtype)


def mla_q_proj(cq, w_uq_cat, rope_tabs, *, seq, tm, heads_per_step):
    n = cq.shape[0]
    wn = w_uq_cat.shape[1]
    tn = heads_per_step * 2 * LANES
    npos = seq // tm
    tab = pl.BlockSpec((tm, LANES), lambda i, j: (i % npos, 0))
    scale = (NOPE_DIM + ROPE_DIM) ** -0.5
    return pl.pallas_call(
        functools.partial(_mla_q_kernel, heads=heads_per_step, scale=scale),
        grid=(n // tm, wn // tn),
        in_specs=[pl.BlockSpec((tm, Q_LORA), lambda i, j: (i, 0)), pl.BlockSpec((Q_LORA, tn), lambda i, j: (0, j)),
                  tab, tab, tab],
        out_specs=pl.BlockSpec((tm, tn), lambda i, j: (i, j)),
        out_shape=jax.ShapeDtypeStruct((n, wn), BF16),
        compiler_params=_cparams(("parallel", "parallel")),
        name="mla_q_proj",
    )(cq, w_uq_cat, *rope_tabs)


def _mla_attn_kernel(q_ref, kn_ref, kr_ref, v_ref, o_ref, m_sc, l_sc, acc_sc, *, tq, tk, nk):
    qi = pl.program_id(2)
    kj = pl.program_id(3)

    @pl.when(kj == 0)
    def _():
        m_sc[...] = jnp.full_like(m_sc, NEG_INF)
        l_sc[...] = jnp.zeros_like(l_sc)
        acc_sc[...] = jnp.zeros_like(acc_sc)

    last = ((qi + 1) * tq - 1) // tk

    @pl.when(kj <= last)
    def _():
        k = jnp.concatenate([kn_ref[...], kr_ref[...]], axis=1)
        s = lax.dot_general(q_ref[...], k, (((1,), (1,)), ((), ())), preferred_element_type=F32)
        q_chunk = jnp.right_shift(qi * tq + lax.broadcasted_iota(I32, (tq, 1), 0), CHUNK_SHIFT)
        k_chunk = jnp.right_shift(kj * tk + lax.broadcasted_iota(I32, (1, tk), 1), CHUNK_SHIFT)
        s = jnp.where(k_chunk <= q_chunk, s, NEG_INF)
        m_prev = m_sc[...]
        m_new = jnp.maximum(m_prev, jnp.max(s, axis=1, keepdims=True))
        alpha = jnp.exp(m_prev - m_new)
        p = jnp.exp(s - m_new)
        l_sc[...] = alpha * l_sc[...] + jnp.sum(p, axis=1, keepdims=True)
        acc_sc[...] = alpha * acc_sc[...] + jnp.dot(p.astype(BF16), v_ref[...], preferred_element_type=F32)
        m_sc[...] = m_new

    @pl.when(kj == nk - 1)
    def _():
        o_ref[...] = (acc_sc[...] / l_sc[...]).astype(o_ref.dtype)


def mla_attention(q_cat, kv, k_rope, *, batch, seq, tq, tk):
    n = q_cat.shape[0]
    nq, nk = seq // tq, seq // tk

    def kidx(qi, kj):
        return jnp.minimum(kj, ((qi + 1) * tq - 1) // tk)

    return pl.pallas_call(
        functools.partial(_mla_attn_kernel, tq=tq, tk=tk, nk=nk),
        grid=(batch, MLA_HEADS, nq, nk),
        in_specs=[
            pl.BlockSpec((tq, 2 * LANES), lambda b, h, qi, kj: (b * nq + qi, h)),
            pl.BlockSpec((tk, NOPE_DIM), lambda b, h, qi, kj: (b * nk + kidx(qi, kj), h)),
            pl.BlockSpec((tk, LANES), lambda b, h, qi, kj: (b * nk + kidx(qi, kj), 0)),
            pl.BlockSpec((tk, V_DIM), lambda b, h, qi, kj: (b * nk + kidx(qi, kj), MLA_HEADS + h)),
        ],
        out_specs=pl.BlockSpec((tq, V_DIM), lambda b, h, qi, kj: (b * nq + qi, h)),
        out_shape=jax.ShapeDtypeStruct((n, MLA_HEADS * V_DIM), BF16),
        scratch_shapes=[pltpu.VMEM((tq, 1), F32), pltpu.VMEM((tq, 1), F32), pltpu.VMEM((tq, V_DIM), F32)],
        compiler_params=_cparams(("parallel", "parallel", "parallel", "arbitrary")),
        name="mla_attention",
    )(q_cat, kv, k_rope, kv)


def rope_tables(seq):
    half = ROPE_DIM // 2
    freq = ROPE_THETA ** (-jnp.arange(half, dtype=F32) / half)
    ang = jnp.arange(seq, dtype=F32)[:, None] * freq[None, :]
    cos, sin = jnp.cos(ang), jnp.sin(ang)
    z = jnp.zeros_like(cos)
    c = jnp.concatenate([cos, cos, z, z], axis=1)
    s_up = jnp.concatenate([z, sin, z, z], axis=1)
    s_dn = jnp.concatenate([-sin, z, z, z], axis=1)
    return c, s_up, s_dn


def _router_kernel(x_ref, w_ref, b_ref, o_ref):
    logits = jnp.dot(x_ref[...].astype(BF16), w_ref[...], preferred_element_type=F32) + b_ref[...]
    tm = logits.shape[0]
    lane = lax.broadcasted_iota(I32, (tm, LANES), 1)
    is_group = lane < N_GROUPS
    lg = jnp.where(is_group, logits, NEG_INF)
    mg = jnp.max(lg, axis=1, keepdims=True)
    g_sel = jnp.min(jnp.where(lg == mg, lane, LANES), axis=1, keepdims=True)
    p_group = 1.0 / jnp.sum(jnp.where(is_group, jnp.exp(lg - mg), 0.0), axis=1, keepdims=True)
    e_lane = lane - N_GROUPS
    in_sel = (e_lane >= 0) & (e_lane < N_EXPERTS) & (jnp.right_shift(e_lane, GROUP_SHIFT) == g_sel)
    le = jnp.where(in_sel, logits, NEG_INF)
    v1 = jnp.max(le, axis=1, keepdims=True)
    i1 = jnp.min(jnp.where(le == v1, lane, LANES), axis=1, keepdims=True)
    le2 = jnp.where(lane == i1, NEG_INF, le)
    v2 = jnp.max(le2, axis=1, keepdims=True)
    i2 = jnp.min(jnp.where(le2 == v2, lane, LANES), axis=1, keepdims=True)
    t = jnp.exp(v2 - v1)
    p1 = 1.0 / (1.0 + t)
    p2 = t / (1.0 + t)
    out = jnp.where(lane == 0, (i1 - N_GROUPS).astype(F32), 0.0)
    out = jnp.where(lane == 1, (i2 - N_GROUPS).astype(F32), out)
    out = jnp.where(lane == 2, p_group * p1, out)
    out = jnp.where(lane == 3, p_group * p2, out)
    o_ref[...] = out


def moe_router(x, w_route, b_route, *, tm):
    n, d = x.shape
    return pl.pallas_call(
        _router_kernel,
        grid=(n // tm,),
        in_specs=[pl.BlockSpec((tm, d), lambda i: (i, 0)), pl.BlockSpec((d, LANES), lambda i: (0, 0)),
                  pl.BlockSpec((1, LANES), lambda i: (0, 0))],
        out_specs=pl.BlockSpec((tm, LANES), lambda i: (i, 0)),
        out_shape=jax.ShapeDtypeStruct((n, LANES), F32),
        compiler_params=_cparams(("parallel",)),
        name="moe_router",
    )(x, w_route, b_route)


def _row_copy(src, src_row, dst, dst_row, sem):
    return pltpu.make_async_copy(src.at[pl.ds(src_row, 1)], dst.at[pl.ds(dst_row, 1)], sem)


def _dispatch_kernel(dest_ref, x_ref, init_ref, xs_ref, sem, *, tm):
    del init_ref
    base = pl.program_id(0) * tm * 2

    def copy(a):
        return _row_copy(x_ref, a // 2, xs_ref, dest_ref[base + a], sem)

    def start(a, c):
        copy(a).start()
        return c

    def wait(a, c):
        copy(a).wait()
        return c

    lax.fori_loop(0, 2 * tm, start, 0)
    lax.fori_loop(0, 2 * tm, wait, 0)


def moe_dispatch(x, dest, rows, *, tm):
    n, d = x.shape
    return pl.pallas_call(
        functools.partial(_dispatch_kernel, tm=tm),
        grid_spec=pltpu.PrefetchScalarGridSpec(
            num_scalar_prefetch=1,
            grid=(n // tm,),
            in_specs=[pl.BlockSpec((tm, d), lambda i, dest: (i, 0)), pl.BlockSpec(memory_space=pl.ANY)],
            out_specs=pl.BlockSpec(memory_space=pl.ANY),
            scratch_shapes=[pltpu.SemaphoreType.DMA(())],
        ),
        out_shape=jax.ShapeDtypeStruct((rows, d), x.dtype),
        input_output_aliases={2: 0},
        compiler_params=_cparams(("arbitrary",)),
        name="moe_dispatch",
    )(dest, x, jnp.zeros((rows, d), x.dtype))


def _expert_kernel(be_ref, nu_ref, xs_ref, wg_ref, wu_ref, wd_ref, o_ref, wg_sc, wu_sc, wd_sc):
    i = pl.program_id(0)
    prev_e = be_ref[jnp.maximum(i - 1, 0)]

    @pl.when((i == 0) | (be_ref[i] != prev_e))
    def _():
        wg_sc[...] = wg_ref[...].astype(BF16)
        wu_sc[...] = wu_ref[...].astype(BF16)
        wd_sc[...] = wd_ref[...].astype(BF16)

    @pl.when(i < nu_ref[0])
    def _():
        xb = xs_ref[...].astype(BF16)
        hg = jnp.dot(xb, wg_sc[...], preferred_element_type=F32)
        hu = jnp.dot(xb, wu_sc[...], preferred_element_type=F32)
        hb = (hg * jax.nn.sigmoid(hg) * hu).astype(BF16)
        o_ref[...] = jnp.dot(hb, wd_sc[...], preferred_element_type=F32)


def moe_experts(xs, block_e, n_used, w_gate, w_up, w_down, *, tb):
    rows, d = xs.shape
    hdim = w_gate.shape[2]
    n_blocks = rows // tb

    def blk(i, be, nu):
        return jnp.minimum(i, nu[0] - 1)

    def wmap(i, be, nu):
        return (be[blk(i, be, nu)], 0, 0)

    return pl.pallas_call(
        _expert_kernel,
        grid_spec=pltpu.PrefetchScalarGridSpec(
            num_scalar_prefetch=2,
            grid=(n_blocks,),
            in_specs=[pl.BlockSpec((tb, d), lambda i, be, nu: (blk(i, be, nu), 0)),
                      pl.BlockSpec((None, d, hdim), wmap), pl.BlockSpec((None, d, hdim), wmap),
                      pl.BlockSpec((None, hdim, d), wmap)],
            out_specs=pl.BlockSpec((tb, d), lambda i, be, nu: (blk(i, be, nu), 0)),
            scratch_shapes=[pltpu.VMEM((d, hdim), BF16), pltpu.VMEM((d, hdim), BF16), pltpu.VMEM((hdim, d), BF16)],
        ),
        out_shape=jax.ShapeDtypeStruct((rows, d), F32),
        compiler_params=_cparams(("arbitrary",)),
        name="moe_experts",
    )(block_e, n_used, xs, w_gate, w_up, w_down)


def _combine_ln_kernel(dest_ref, x_ref, route_ref, g_ref, b_ref, ys_ref, o_ref, buf, sem, *, tm):
    base = pl.program_id(0) * tm * 2

    def copy(a):
        return _row_copy(ys_ref, dest_ref[base + a], buf, (a % 2) * tm + a // 2, sem)

    def start(a, c):
        copy(a).start()
        return c

    def wait(a, c):
        copy(a).wait()
        return c

    lax.fori_loop(0, 2 * tm, start, 0)
    lax.fori_loop(0, 2 * tm, wait, 0)
    g1 = route_ref[:, 2:3]
    g2 = route_ref[:, 3:4]
    f = buf[:tm, :] * g1 + buf[tm:, :] * g2
    z = DEEPNORM_ALPHA * x_ref[...] + f
    o_ref[...] = _layer_norm_rows(z, g_ref[...], b_ref[...])


def moe_combine_ln(x, route, dest, ys, g, b, *, tm):
    n, d = x.shape
    return pl.pallas_call(
        functools.partial(_combine_ln_kernel, tm=tm),
        grid_spec=pltpu.PrefetchScalarGridSpec(
            num_scalar_prefetch=1,
            grid=(n // tm,),
            in_specs=[pl.BlockSpec((tm, d), lambda i, dest: (i, 0)), pl.BlockSpec((tm, LANES), lambda i, dest: (i, 0)),
                      pl.BlockSpec((1, d), lambda i, dest: (0, 0)), pl.BlockSpec((1, d), lambda i, dest: (0, 0)),
                      pl.BlockSpec(memory_space=pl.ANY)],
            out_specs=pl.BlockSpec((tm, d), lambda i, dest: (i, 0)),
            scratch_shapes=[pltpu.VMEM((2 * tm, d), F32), pltpu.SemaphoreType.DMA(())],
        ),
        out_shape=jax.ShapeDtypeStruct((n, d), F32),
        compiler_params=_cparams(("arbitrary",)),
        name="moe_combine_ln",
    )(dest, x, route, g.reshape(1, d), b.reshape(1, d), ys)


def routing_tables(route, *, tb):
    n = route.shape[0]
    e_flat = route[:, :2].astype(I32).reshape(-1)
    onehot = (e_flat[:, None] == jnp.arange(N_EXPERTS, dtype=I32)[None, :]).astype(I32)
    csum = jnp.cumsum(onehot, axis=0)
    rank = jnp.sum(csum * onehot, axis=1) - 1
    counts = csum[-1]
    padded = ((counts + tb - 1) // tb) * tb
    pends = jnp.cumsum(padded)
    pstarts = pends - padded
    dest = jnp.sum(onehot * pstarts[None, :], axis=1) + rank
    n_blocks = (2 * n) // tb + N_EXPERTS
    block_start = jnp.arange(n_blocks, dtype=I32) * tb
    block_e = jnp.minimum(jnp.searchsorted(pends, block_start, side="right"), N_EXPERTS - 1).astype(I32)
    n_used = (pends[-1] // tb).astype(I32).reshape(1)
    return dest.astype(I32), block_e, n_used, n_blocks * tb


def hier_moe_ln(x, w_route, b_route, w_gate, w_up, w_down, g, b, *, tb=256, tm=256):
    route = moe_router(x, w_route, b_route, tm=512)
    dest, block_e, n_used, rows = routing_tables(route, tb=tb)
    xs = moe_dispatch(x, dest, rows, tm=tm)
    ys = moe_experts(xs, block_e, n_used, w_gate, w_up, w_down, tb=tb)
    return moe_combine_ln(x, route, dest, ys, g, b, tm=tm)


def pool_band_layer(x, w_in, w_pool, pool_scale, rel_bias, w_o, g, b, *, batch, seq):
    qscale = jnp.concatenate([jnp.ones((POOL_WIDTH,), F32), jnp.full((ATT_WIDTH,), ATT_HEAD_DIM ** -0.5, F32),
                              jnp.ones((2 * ATT_WIDTH,), F32)])
    h = matmul(x, (w_in * qscale).astype(BF16), tm=512, tn=1024, out_dtype=BF16)
    a_out = pool_mixer(h, w_pool.astype(BF16), pool_scale.reshape(1, -1), seq=seq, tm=512)
    b_out = band_attention(h, band_bias_table(rel_bias), batch=batch, seq=seq, tq=min(512, seq))
    w_o = w_o.astype(BF16)
    return proj_residual_ln([a_out, b_out], [w_o[:POOL_WIDTH], w_o[POOL_WIDTH:]], x, g, b, tm=256)


def mla_layer(x, w_in, g_q, g_kv, w_uq, w_uk, w_uv, w_o, g, b, *, batch, seq):
    d = x.shape[1]
    tabs = rope_tables(seq)
    w_in_pad = jnp.concatenate([w_in, jnp.zeros((d, LANES - ROPE_DIM), F32)], axis=1).astype(BF16)
    cq, ckv, k_rope = mla_in_proj(x, w_in_pad, g_q, g_kv, tabs, seq=seq, tm=512)
    w_uq_h = w_uq.reshape(Q_LORA, MLA_HEADS, NOPE_DIM + ROPE_DIM)
    w_uq_cat = jnp.concatenate([w_uq_h, jnp.zeros((Q_LORA, MLA_HEADS, 2 * LANES - NOPE_DIM - ROPE_DIM), F32)], axis=2)
    w_uq_cat = w_uq_cat.reshape(Q_LORA, MLA_HEADS * 2 * LANES).astype(BF16)
    q_cat = mla_q_proj(cq, w_uq_cat, tabs, seq=seq, tm=512, heads_per_step=4)
    kv = matmul(ckv, jnp.concatenate([w_uk, w_uv], axis=1).astype(BF16), tm=512, tn=1024, out_dtype=BF16)
    o = mla_attention(q_cat, kv, k_rope, batch=batch, seq=seq, tq=min(512, seq), tk=min(512, seq))
    return proj_residual_ln([o], [w_o.astype(BF16)], x, g, b, tm=256)


def kernel(x, ab_w_in, ab_w_pool, ab_pool_scale, ab_rel_bias, ab_w_o, mla_w_in, mla_g_q, mla_g_kv, mla_w_uq, mla_w_uk, mla_w_uv, mla_w_o, ln1_g, ln1_b, moe_w_group, moe_b_group, moe_w_expert, moe_b_expert, moe_w_gate, moe_w_up, moe_w_down, ln2_g, ln2_b):
    batch, seq, d = x.shape
    x = x.reshape(batch * seq, d)
    for i in range(DEPTH):
        j = i // 2
        if i % 2 == 0:
            x = pool_band_layer(x, ab_w_in[j], ab_w_pool[j], ab_pool_scale[j], ab_rel_bias[j], ab_w_o[j],
                                ln1_g[i], ln1_b[i], batch=batch, seq=seq)
        else:
            x = mla_layer(x, mla_w_in[j], mla_g_q[j], mla_g_kv[j], mla_w_uq[j], mla_w_uk[j], mla_w_uv[j], mla_w_o[j],
                          ln1_g[i], ln1_b[i], batch=batch, seq=seq)
        pad = jnp.zeros((d, LANES - N_GROUPS - N_EXPERTS), F32)
        w_route = jnp.concatenate([moe_w_group[i], moe_w_expert[i], pad], axis=1).astype(BF16)
        b_route = jnp.concatenate([moe_b_group[i], moe_b_expert[i], jnp.zeros((LANES - N_GROUPS - N_EXPERTS,), F32)])
        x = hier_moe_ln(x, w_route, b_route.reshape(1, LANES), moe_w_gate[i], moe_w_up[i], moe_w_down[i],
                        ln2_g[i], ln2_b[i])
    return x.reshape(batch, seq, d)
```

```python
import functools

import jax
import jax.numpy as jnp
from jax import lax
from jax.experimental import pallas as pl
from jax.experimental.pallas import tpu as pltpu

F32 = jnp.float32
BF16 = jnp.bfloat16
I32 = jnp.int32
U32 = jnp.uint32

D_MODEL = 2048
DEPTH = 4
CHUNK = 64
CHUNK_SHIFT = 6
GROUP_SHIFT = 3
POOL_WINDOWS = (2, 4, 8, 16)
POOL_GROUP_DIM = 256
POOL_WIDTH = 1024
ATT_WIDTH = 1024
ATT_HEADS = 16
ATT_HEAD_DIM = 64
LEFT_CHUNKS = 8
REL_MAX = 128
MLA_HEADS = 16
Q_LORA = 512
KV_LORA = 512
NOPE_DIM = 128
ROPE_DIM = 64
V_DIM = 128
ROPE_THETA = 10000.0
N_GROUPS = 8
EXPERTS_PER_GROUP = 8
N_EXPERTS = 64
EXPERT_HIDDEN = 512
DEEPNORM_ALPHA = (2 * DEPTH) ** 0.25
LN_EPS = 1e-5
RMS_EPS = 1e-6

LANES = 128
PACK_ROWS = 8
POOL_HALO = 16
BAND_SUB = 128
BAND_KEYS = BAND_SUB + LEFT_CHUNKS * CHUNK
NEG_INF = float("-inf")
LOG2_E = 1.4426950408889634
VMEM_LIMIT = 56 * 1024 * 1024


def _cparams(sem, vmem=VMEM_LIMIT):
    return pltpu.CompilerParams(dimension_semantics=sem, vmem_limit_bytes=vmem)


def _mm_kernel(x_ref, w_ref, o_ref):
    o_ref[...] = jnp.dot(x_ref[...].astype(BF16), w_ref[...], preferred_element_type=F32).astype(o_ref.dtype)


def matmul(x, w, *, tm, tn, out_dtype):
    m, k = x.shape
    n = w.shape[1]
    return pl.pallas_call(
        _mm_kernel,
        grid=(m // tm, n // tn),
        in_specs=[pl.BlockSpec((tm, k), lambda i, j: (i, 0)), pl.BlockSpec((k, tn), lambda i, j: (0, j))],
        out_specs=pl.BlockSpec((tm, tn), lambda i, j: (i, j)),
        out_shape=jax.ShapeDtypeStruct((m, n), out_dtype),
        compiler_params=_cparams(("parallel", "parallel")),
        name="matmul",
    )(x, w)


def _pool_kernel(u_ref, halo_ref, w_ref, scale_ref, o_ref, *, tm, seq):
    i = pl.program_id(0)
    pos0 = (i * tm) % seq
    u = u_ref[...].astype(F32)
    halo = jnp.where(pos0 == 0, 0.0, halo_ref[...].astype(F32))
    ue = jnp.concatenate([halo, u], axis=0)
    pos = pos0 + lax.broadcasted_iota(I32, (tm, 1), 0)
    for g, win in enumerate(POOL_WINDOWS):
        cols = slice(g * POOL_GROUP_DIM, (g + 1) * POOL_GROUP_DIM)
        s = ue[:, cols]
        shift = 1
        while shift < win:
            s = s + pltpu.roll(s, shift, axis=0)
            shift *= 2
        wsum = s[POOL_HALO:, :]
        count = jnp.minimum(pos + 1, win).astype(F32)
        d = wsum / count - u[:, cols]
        y = jnp.dot(d.astype(BF16), w_ref[g], preferred_element_type=F32) * scale_ref[:, cols]
        o_ref[:, cols] = y.astype(o_ref.dtype)


def pool_mixer(h, w_pool, pool_scale, *, seq, tm):
    n = h.shape[0]
    hb = tm // POOL_HALO
    return pl.pallas_call(
        functools.partial(_pool_kernel, tm=tm, seq=seq),
        grid=(n // tm,),
        in_specs=[
            pl.BlockSpec((tm, POOL_WIDTH), lambda i: (i, 0)),
            pl.BlockSpec((POOL_HALO, POOL_WIDTH), lambda i: (jnp.maximum(i * hb - 1, 0), 0)),
            pl.BlockSpec((len(POOL_WINDOWS), POOL_GROUP_DIM, POOL_GROUP_DIM), lambda i: (0, 0, 0)),
            pl.BlockSpec((1, POOL_WIDTH), lambda i: (0, 0)),
        ],
        out_specs=pl.BlockSpec((tm, POOL_WIDTH), lambda i: (i, 0)),
        out_shape=jax.ShapeDtypeStruct((n, POOL_WIDTH), BF16),
        compiler_params=_cparams(("parallel",)),
        name="pool_mixer",
    )(h, h, w_pool, pool_scale)


def _band_kernel(q_ref, kp_ref, kc_ref, vp_ref, vc_ref, bias_ref, o_ref, *, tq):
    i = pl.program_id(1)
    kk = jnp.concatenate([kp_ref[...], kc_ref[...]], axis=0)
    vv = jnp.concatenate([vp_ref[...], vc_ref[...]], axis=0)
    lane = lax.broadcasted_iota(I32, (1, LANES), 1)
    first_head = lane < ATT_HEAD_DIM
    col = lax.broadcasted_iota(I32, (1, BAND_KEYS), 1)
    left = LEFT_CHUNKS * CHUNK
    ones = jnp.ones((BAND_KEYS, LANES), BF16)
    for j in range(tq // BAND_SUB):
        qs = q_ref[j * BAND_SUB:(j + 1) * BAND_SUB, :]
        off = tq - left + j * BAND_SUB
        kw = kk[off:off + BAND_KEYS, :]
        vw = jnp.concatenate([vv[off:off + BAND_KEYS, :], ones], axis=1)
        valid = (i * tq + j * BAND_SUB - left + col) >= 0
        outs = []
        for hh in range(2):
            qm = jnp.where(first_head if hh == 0 else jnp.logical_not(first_head), qs, jnp.zeros_like(qs))
            s = lax.dot_general(qm, kw, (((1,), (1,)), ((), ())), preferred_element_type=F32)
            s = jnp.where(valid, s + bias_ref[hh], NEG_INF)
            m = jnp.max(s, axis=1, keepdims=True)
            p = jnp.exp2(s - m)
            o = jnp.dot(p.astype(BF16), vw, preferred_element_type=F32)
            outs.append(o[:, :LANES] / o[:, LANES:])
        o_ref[j * BAND_SUB:(j + 1) * BAND_SUB, :] = jnp.where(first_head, outs[0], outs[1]).astype(o_ref.dtype)


def band_bias_table(rel_bias):
    heads, rel_size = rel_bias.shape
    r = jnp.arange(BAND_SUB, dtype=I32)[:, None]
    c = jnp.arange(BAND_KEYS, dtype=I32)[None, :]
    qc, kc = r // CHUNK, c // CHUNK
    allowed = (kc >= qc) & (kc <= qc + LEFT_CHUNKS)
    n_lo = BAND_SUB - CHUNK
    n_hi = BAND_KEYS - 1 - REL_MAX
    by_dist = jnp.concatenate([jnp.repeat(rel_bias[:, :1], n_lo, axis=1), rel_bias,
                               jnp.repeat(rel_bias[:, -1:], n_hi, axis=1)], axis=1).astype(F32)
    period = BAND_SUB + BAND_KEYS - 1
    w = jnp.concatenate([by_dist[:, :BAND_KEYS][:, ::-1], by_dist[:, BAND_KEYS:][:, ::-1]], axis=1)
    rows = jnp.tile(w, (1, BAND_SUB))[:, :BAND_SUB * (period - 1)].reshape(heads, BAND_SUB, period - 1)
    return jnp.where(allowed[None], rows[:, :, :BAND_KEYS] * LOG2_E, NEG_INF)


def band_attention(h, bias_tab, *, batch, seq, tq):
    n = h.shape[0]
    nq = seq // tq
    qcol = POOL_WIDTH // LANES
    kcol = qcol + ATT_WIDTH // LANES
    vcol = kcol + ATT_WIDTH // LANES
    cur = lambda c0: (lambda b, i, p: (b * nq + i, c0 + p))
    prev = lambda c0: (lambda b, i, p: (b * nq + jnp.maximum(i - 1, 0), c0 + p))
    blk = (tq, LANES)
    return pl.pallas_call(
        functools.partial(_band_kernel, tq=tq),
        grid=(batch, nq, ATT_HEADS // 2),
        in_specs=[
            pl.BlockSpec(blk, cur(qcol)),
            pl.BlockSpec(blk, prev(kcol)), pl.BlockSpec(blk, cur(kcol)),
            pl.BlockSpec(blk, prev(vcol)), pl.BlockSpec(blk, cur(vcol)),
            pl.BlockSpec((2, BAND_SUB, BAND_KEYS), lambda b, i, p: (p, 0, 0)),
        ],
        out_specs=pl.BlockSpec(blk, lambda b, i, p: (b * nq + i, p)),
        out_shape=jax.ShapeDtypeStruct((n, ATT_WIDTH), BF16),
        compiler_params=_cparams(("parallel", "parallel", "parallel")),
        name="band_attention",
    )(h, h, h, h, h, bias_tab)


def _layer_norm_rows(z, g, b):
    mu = jnp.mean(z, axis=-1, keepdims=True)
    zc = z - mu
    var = jnp.mean(zc * zc, axis=-1, keepdims=True)
    return zc * lax.rsqrt(var + LN_EPS) * g + b


def _pack_tokens(z, ref):
    tm, d = z.shape
    lo = lax.bitcast_convert_type(z[:, :d // 2].astype(BF16).astype(F32), U32)
    hi = lax.bitcast_convert_type(z[:, d // 2:].astype(BF16).astype(F32), U32)
    pk = lax.shift_right_logical(lo, jnp.uint32(16)) | (hi & jnp.uint32(0xFFFF0000))
    for s in range(PACK_ROWS):
        ref[pl.ds(s, tm, stride=PACK_ROWS), :] = pk[:, s * LANES:(s + 1) * LANES]


def _unpack_tokens(ref, first_row, tm):
    pk = jnp.concatenate([ref[pl.ds(first_row + s, tm, stride=PACK_ROWS), :] for s in range(PACK_ROWS)], axis=1)
    lo = lax.bitcast_convert_type(lax.shift_left(pk, jnp.uint32(16)), F32)
    hi = lax.bitcast_convert_type(pk & jnp.uint32(0xFFFF0000), F32)
    return lo, hi


def _proj_ln_kernel(*refs, n_in):
    a_refs, w_refs = refs[:n_in], refs[n_in:2 * n_in]
    x_ref, g_ref, b_ref, wr_ref, br_ref, o_ref, opk_ref, route_ref = refs[2 * n_in:]
    y = jnp.dot(a_refs[0][...], w_refs[0][...], preferred_element_type=F32)
    for a_ref, w_ref in zip(a_refs[1:], w_refs[1:]):
        y = y + jnp.dot(a_ref[...], w_ref[...], preferred_element_type=F32)
    z = DEEPNORM_ALPHA * x_ref[...] + y
    xn = _layer_norm_rows(z, g_ref[...], b_ref[...])
    o_ref[...] = xn
    _pack_tokens(xn, opk_ref)
    route_ref[...] = _route_rows(xn, wr_ref, br_ref)


def proj_residual_ln(acts, weights, x, g, b, w_route, b_route, *, tm):
    n, d = x.shape
    n_in = len(acts)
    const = lambda i: (0, 0)
    in_specs = [pl.BlockSpec((tm, a.shape[1]), lambda i: (i, 0)) for a in acts]
    in_specs += [pl.BlockSpec(w.shape, const) for w in weights]
    in_specs += [pl.BlockSpec((tm, d), lambda i: (i, 0)), pl.BlockSpec((1, d), const), pl.BlockSpec((1, d), const),
                 pl.BlockSpec((d, LANES), const), pl.BlockSpec((1, LANES), const)]
    return pl.pallas_call(
        functools.partial(_proj_ln_kernel, n_in=n_in),
        grid=(n // tm,),
        in_specs=in_specs,
        out_specs=[pl.BlockSpec((tm, d), lambda i: (i, 0)), pl.BlockSpec((tm * PACK_ROWS, LANES), lambda i: (i, 0)),
                   pl.BlockSpec((tm, LANES), lambda i: (i, 0))],
        out_shape=[jax.ShapeDtypeStruct((n, d), F32), jax.ShapeDtypeStruct((n * PACK_ROWS, LANES), U32),
                   jax.ShapeDtypeStruct((n, LANES), F32)],
        compiler_params=_cparams(("parallel",)),
        name="proj_residual_ln",
    )(*acts, *weights, x, g.reshape(1, d), b.reshape(1, d), w_route, b_route)


def _rope_lanes(x, c_ref, s_up_ref, s_dn_ref):
    half = ROPE_DIM // 2
    return (x * c_ref[...] + pltpu.roll(x, half, axis=1) * s_up_ref[...]
            + pltpu.roll(x, LANES - half, axis=1) * s_dn_ref[...])


def _rms_rows(x, g):
    return x * lax.rsqrt(jnp.mean(x * x, axis=-1, keepdims=True) + RMS_EPS) * g


def _mla_in_kernel(x_ref, w_ref, gq_ref, gkv_ref, c_ref, su_ref, sd_ref, cq_ref, ckv_ref, kr_ref):
    h = jnp.dot(x_ref[...].astype(BF16), w_ref[...], preferred_element_type=F32)
    cq_ref[...] = _rms_rows(h[:, :Q_LORA], gq_ref[...]).astype(cq_ref.dtype)
    ckv_ref[...] = _rms_rows(h[:, Q_LORA:Q_LORA + KV_LORA], gkv_ref[...]).astype(ckv_ref.dtype)
    kr_ref[...] = _rope_lanes(h[:, Q_LORA + KV_LORA:], c_ref, su_ref, sd_ref).astype(kr_ref.dtype)


def mla_in_proj(x, w_in_pad, g_q, g_kv, rope_tabs, *, seq, tm):
    n, d = x.shape
    wn = w_in_pad.shape[1]
    npos = seq // tm
    tab = pl.BlockSpec((tm, LANES), lambda i: (i % npos, 0))
    return pl.pallas_call(
        _mla_in_kernel,
        grid=(n // tm,),
        in_specs=[pl.BlockSpec((tm, d), lambda i: (i, 0)), pl.BlockSpec((d, wn), lambda i: (0, 0)),
                  pl.BlockSpec((1, Q_LORA), lambda i: (0, 0)), pl.BlockSpec((1, KV_LORA), lambda i: (0, 0)),
                  tab, tab, tab],
        out_specs=[pl.BlockSpec((tm, Q_LORA), lambda i: (i, 0)), pl.BlockSpec((tm, KV_LORA), lambda i: (i, 0)),
                   pl.BlockSpec((tm, LANES), lambda i: (i, 0))],
        out_shape=[jax.ShapeDtypeStruct((n, Q_LORA), BF16), jax.ShapeDtypeStruct((n, KV_LORA), BF16),
                   jax.ShapeDtypeStruct((n, LANES), BF16)],
        compiler_params=_cparams(("parallel",)),
        name="mla_in_proj",
    )(x, w_in_pad, g_q.reshape(1, -1), g_kv.reshape(1, -1), *rope_tabs)


def _mla_q_kernel(cq_ref, w_ref, c_ref, su_ref, sd_ref, o_ref, *, heads, scale):
    q = jnp.dot(cq_ref[...], w_ref[...], preferred_element_type=F32)
    for hh in range(heads):
        base = hh * 2 * LANES
        o_ref[:, base:base + LANES] = (q[:, base:base + LANES] * scale).astype(o_ref.dtype)
        rot = _rope_lanes(q[:, base + LANES:base + 2 * LANES], c_ref, su_ref, sd_ref)
        o_ref[:, base + LANES:base + 2 * LANES] = (rot * scale).astype(o_ref.dtype)


def mla_q_proj(cq, w_uq_cat, rope_tabs, *, seq, tm, heads_per_step):
    n = cq.shape[0]
    wn = w_uq_cat.shape[1]
    tn = heads_per_step * 2 * LANES
    npos = seq // tm
    tab = pl.BlockSpec((tm, LANES), lambda i, j: (i % npos, 0))
    scale = (NOPE_DIM + ROPE_DIM) ** -0.5 * LOG2_E
    return pl.pallas_call(
        functools.partial(_mla_q_kernel, heads=heads_per_step, scale=scale),
        grid=(n // tm, wn // tn),
        in_specs=[pl.BlockSpec((tm, Q_LORA), lambda i, j: (i, 0)), pl.BlockSpec((Q_LORA, tn), lambda i, j: (0, j)),
                  tab, tab, tab],
        out_specs=pl.BlockSpec((tm, tn), lambda i, j: (i, j)),
        out_shape=jax.ShapeDtypeStruct((n, wn), BF16),
        compiler_params=_cparams(("parallel", "parallel")),
        name="mla_q_proj",
    )(cq, w_uq_cat, *rope_tabs)


def _mla_attn_kernel(q_ref, kn_ref, kr_ref, v_ref, o_ref, m_sc, acc_sc, *, tq, tk, heads):
    qi = pl.program_id(2)
    n_full = qi * (tq // tk)
    m_sc[...] = jnp.full_like(m_sc, NEG_INF)
    acc_sc[...] = jnp.zeros_like(acc_sc)
    ones = jnp.ones((tk, LANES), BF16)
    row_chunk = jnp.right_shift(lax.broadcasted_iota(I32, (tq, 1), 0), CHUNK_SHIFT)
    col_chunk = jnp.right_shift(lax.broadcasted_iota(I32, (1, tk), 1), CHUNK_SHIFT)

    def step(kj, diag):
        off = pl.multiple_of(kj * tk, tk)
        kr = kr_ref[pl.ds(off, tk), :]
        for hh in range(heads):
            k = jnp.concatenate([kn_ref[pl.ds(off, tk), hh * NOPE_DIM:(hh + 1) * NOPE_DIM], kr], axis=1)
            s = lax.dot_general(q_ref[:, hh * 2 * LANES:(hh + 1) * 2 * LANES], k, (((1,), (1,)), ((), ())),
                                preferred_element_type=F32)
            if diag is not None:
                s = jnp.where(col_chunk + diag * (tk // CHUNK) <= row_chunk, s, NEG_INF)
            m_prev = m_sc[hh]
            m_new = jnp.maximum(m_prev, jnp.max(s, axis=1, keepdims=True))
            alpha = jnp.exp2(m_prev - m_new)
            p = jnp.exp2(s - jnp.tile(m_new, (1, tk // LANES)))
            va = jnp.concatenate([v_ref[pl.ds(off, tk), hh * V_DIM:(hh + 1) * V_DIM], ones], axis=1)
            pv = jnp.dot(p.astype(BF16), va, preferred_element_type=F32)
            acc_sc[hh] = jnp.tile(alpha, (1, 2)) * acc_sc[hh] + pv
            m_sc[hh] = m_new

    def full_step(kj, c):
        step(kj, None)
        return c

    lax.fori_loop(0, n_full, full_step, 0)
    for dj in range(tq // tk):
        step(n_full + dj, dj)
    for hh in range(heads):
        acc = acc_sc[hh]
        o_ref[:, hh * V_DIM:(hh + 1) * V_DIM] = (acc[:, :V_DIM] / acc[:, V_DIM:]).astype(o_ref.dtype)


def mla_attention(q_cat, kv, k_rope, *, batch, seq, tq, tk, heads):
    n = q_cat.shape[0]
    nq = seq // tq
    n_hb = MLA_HEADS // heads
    return pl.pallas_call(
        functools.partial(_mla_attn_kernel, tq=tq, tk=tk, heads=heads),
        grid=(batch, n_hb, nq),
        in_specs=[
            pl.BlockSpec((tq, heads * 2 * LANES), lambda b, h, qi: (b * nq + qi, h)),
            pl.BlockSpec((seq, heads * NOPE_DIM), lambda b, h, qi: (b, h)),
            pl.BlockSpec((seq, LANES), lambda b, h, qi: (b, 0)),
            pl.BlockSpec((seq, heads * V_DIM), lambda b, h, qi: (b, n_hb + h)),
        ],
        out_specs=pl.BlockSpec((tq, heads * V_DIM), lambda b, h, qi: (b * nq + qi, h)),
        out_shape=jax.ShapeDtypeStruct((n, MLA_HEADS * V_DIM), BF16),
        scratch_shapes=[pltpu.VMEM((heads, tq, LANES), F32), pltpu.VMEM((heads, tq, 2 * LANES), F32)],
        compiler_params=_cparams(("parallel", "parallel", "arbitrary")),
        name="mla_attention",
    )(q_cat, kv, k_rope, kv)


def rope_tables(seq):
    half = ROPE_DIM // 2
    freq = ROPE_THETA ** (-jnp.arange(half, dtype=F32) / half)
    ang = jnp.arange(seq, dtype=F32)[:, None] * freq[None, :]
    cos, sin = jnp.cos(ang), jnp.sin(ang)
    z = jnp.zeros_like(cos)
    c = jnp.concatenate([cos, cos, z, z], axis=1)
    s_up = jnp.concatenate([z, sin, z, z], axis=1)
    s_dn = jnp.concatenate([-sin, z, z, z], axis=1)
    return c, s_up, s_dn


def _route_rows(xn, w_ref, b_ref):
    logits = jnp.dot(xn.astype(BF16), w_ref[...], preferred_element_type=F32) + b_ref[...]
    tm = logits.shape[0]
    lane = lax.broadcasted_iota(I32, (tm, LANES), 1)
    is_group = lane < N_GROUPS
    lg = jnp.where(is_group, logits, NEG_INF)
    mg = jnp.max(lg, axis=1, keepdims=True)
    g_sel = jnp.min(jnp.where(lg == mg, lane, LANES), axis=1, keepdims=True)
    p_group = 1.0 / jnp.sum(jnp.where(is_group, jnp.exp(lg - mg), 0.0), axis=1, keepdims=True)
    e_lane = lane - N_GROUPS
    in_sel = (e_lane >= 0) & (e_lane < N_EXPERTS) & (jnp.right_shift(e_lane, GROUP_SHIFT) == g_sel)
    le = jnp.where(in_sel, logits, NEG_INF)
    v1 = jnp.max(le, axis=1, keepdims=True)
    i1 = jnp.min(jnp.where(le == v1, lane, LANES), axis=1, keepdims=True)
    le2 = jnp.where(lane == i1, NEG_INF, le)
    v2 = jnp.max(le2, axis=1, keepdims=True)
    i2 = jnp.min(jnp.where(le2 == v2, lane, LANES), axis=1, keepdims=True)
    t = jnp.exp(v2 - v1)
    p1 = 1.0 / (1.0 + t)
    p2 = t / (1.0 + t)
    out = jnp.where(lane == 0, (i1 - N_GROUPS).astype(F32), 0.0)
    out = jnp.where(lane == 1, (i2 - N_GROUPS).astype(F32), out)
    out = jnp.where(lane == 2, p_group * p1, out)
    out = jnp.where(lane == 3, p_group * p2, out)
    return out


def _token_copy(src, src_tok, dst, dst_tok, sem):
    src_row = pl.multiple_of(src_tok * PACK_ROWS, PACK_ROWS)
    dst_row = pl.multiple_of(dst_tok * PACK_ROWS, PACK_ROWS)
    return pltpu.make_async_copy(src.at[pl.ds(src_row, PACK_ROWS)], dst.at[pl.ds(dst_row, PACK_ROWS)], sem)


def _dispatch_kernel(dest_ref, x_ref, init_ref, xs_ref, sem, *, tm):
    del init_ref
    base = pl.program_id(0) * tm * 2

    def copy(a):
        return _token_copy(x_ref, a // 2, xs_ref, dest_ref[base + a], sem)

    def start(a, c):
        copy(a).start()
        return c

    def wait(a, c):
        copy(a).wait()
        return c

    lax.fori_loop(0, 2 * tm, start, 0)
    lax.fori_loop(0, 2 * tm, wait, 0)


def moe_dispatch(x_pk, dest, rows, *, tm):
    n = x_pk.shape[0] // PACK_ROWS
    return pl.pallas_call(
        functools.partial(_dispatch_kernel, tm=tm),
        grid_spec=pltpu.PrefetchScalarGridSpec(
            num_scalar_prefetch=1,
            grid=(n // tm,),
            in_specs=[pl.BlockSpec((tm * PACK_ROWS, LANES), lambda i, dest: (i, 0)), pl.BlockSpec(memory_space=pl.ANY)],
            out_specs=pl.BlockSpec(memory_space=pl.ANY),
            scratch_shapes=[pltpu.SemaphoreType.DMA(())],
        ),
        out_shape=jax.ShapeDtypeStruct((rows * PACK_ROWS, LANES), U32),
        input_output_aliases={2: 0},
        compiler_params=_cparams(("arbitrary",)),
        name="moe_dispatch",
    )(dest, x_pk, jnp.zeros((rows * PACK_ROWS, LANES), U32))


def _expert_kernel(be_ref, nu_ref, xs_ref, wg_ref, wu_ref, wd_ref, o_ref, wg_sc, wu_sc, wd_sc):
    i = pl.program_id(0)
    prev_e = be_ref[jnp.maximum(i - 1, 0)]

    @pl.when((i == 0) | (be_ref[i] != prev_e))
    def _():
        wg_sc[...] = wg_ref[...].astype(BF16)
        wu_sc[...] = wu_ref[...].astype(BF16)
        wd_sc[...] = wd_ref[...].astype(BF16)

    tb = xs_ref.shape[0] // PACK_ROWS

    @pl.when(i < nu_ref[0])
    def _():
        lo, hi = _unpack_tokens(xs_ref, 0, tb)
        xb = jnp.concatenate([lo.astype(BF16), hi.astype(BF16)], axis=1)
        hg = jnp.dot(xb, wg_sc[...], preferred_element_type=F32)
        hu = jnp.dot(xb, wu_sc[...], preferred_element_type=F32)
        hb = (hg * jax.nn.sigmoid(hg) * hu).astype(BF16)
        _pack_tokens(jnp.dot(hb, wd_sc[...], preferred_element_type=F32), o_ref)

    @pl.when(i >= nu_ref[0])
    def _():
        o_ref[...] = jnp.zeros_like(o_ref)


def moe_experts(xs, block_e, n_used, w_gate, w_up, w_down, *, layer, tb):
    rows = xs.shape[0] // PACK_ROWS
    _, _, d, hdim = w_gate.shape
    n_blocks = rows // tb

    def wmap(i, be, nu):
        return (layer, be[jnp.minimum(i, nu[0] - 1)], 0, 0)

    def xmap(i, be, nu):
        return (jnp.minimum(i, nu[0] - 1), 0)

    return pl.pallas_call(
        _expert_kernel,
        grid_spec=pltpu.PrefetchScalarGridSpec(
            num_scalar_prefetch=2,
            grid=(n_blocks,),
            in_specs=[pl.BlockSpec((tb * PACK_ROWS, LANES), xmap),
                      pl.BlockSpec((None, None, d, hdim), wmap), pl.BlockSpec((None, None, d, hdim), wmap),
                      pl.BlockSpec((None, None, hdim, d), wmap)],
            out_specs=pl.BlockSpec((tb * PACK_ROWS, LANES), lambda i, be, nu: (i, 0)),
            scratch_shapes=[pltpu.VMEM((d, hdim), BF16), pltpu.VMEM((d, hdim), BF16), pltpu.VMEM((hdim, d), BF16)],
        ),
        out_shape=jax.ShapeDtypeStruct((rows * PACK_ROWS, LANES), U32),
        compiler_params=_cparams(("arbitrary",)),
        name="moe_experts",
    )(block_e, n_used, xs, w_gate, w_up, w_down)


def _combine_ln_kernel(dest_ref, x_ref, route_ref, g_ref, b_ref, ys_ref, o_ref, buf, sem, *, tm):
    base = pl.program_id(0) * tm * 2

    def copy(a):
        return _token_copy(ys_ref, dest_ref[base + a], buf, (a % 2) * tm + a // 2, sem)

    def start(a, c):
        copy(a).start()
        return c

    def wait(a, c):
        copy(a).wait()
        return c

    lax.fori_loop(0, 2 * tm, start, 0)
    lax.fori_loop(0, 2 * tm, wait, 0)
    g1 = route_ref[:, 2:3]
    g2 = route_ref[:, 3:4]
    lo1, hi1 = _unpack_tokens(buf, 0, tm)
    lo2, hi2 = _unpack_tokens(buf, tm * PACK_ROWS, tm)
    f = jnp.concatenate([lo1 * g1 + lo2 * g2, hi1 * g1 + hi2 * g2], axis=1)
    z = DEEPNORM_ALPHA * x_ref[...] + f
    o_ref[...] = _layer_norm_rows(z, g_ref[...], b_ref[...])


def moe_combine_ln(x, route, dest, ys, g, b, *, tm):
    n, d = x.shape
    return pl.pallas_call(
        functools.partial(_combine_ln_kernel, tm=tm),
        grid_spec=pltpu.PrefetchScalarGridSpec(
            num_scalar_prefetch=1,
            grid=(n // tm,),
            in_specs=[pl.BlockSpec((tm, d), lambda i, dest: (i, 0)), pl.BlockSpec((tm, LANES), lambda i, dest: (i, 0)),
                      pl.BlockSpec((1, d), lambda i, dest: (0, 0)), pl.BlockSpec((1, d), lambda i, dest: (0, 0)),
                      pl.BlockSpec(memory_space=pl.ANY)],
            out_specs=pl.BlockSpec((tm, d), lambda i, dest: (i, 0)),
            scratch_shapes=[pltpu.VMEM((2 * tm * PACK_ROWS, LANES), U32), pltpu.SemaphoreType.DMA(())],
        ),
        out_shape=jax.ShapeDtypeStruct((n, d), F32),
        compiler_params=_cparams(("arbitrary",)),
        name="moe_combine_ln",
    )(dest, x, route, g.reshape(1, d), b.reshape(1, d), ys)


def routing_tables(route, *, tb):
    n = route.shape[0]
    e_flat = route[:, :2].astype(I32).reshape(-1)
    onehot = (e_flat[:, None] == jnp.arange(N_EXPERTS, dtype=I32)[None, :]).astype(I32)
    csum = jnp.cumsum(onehot, axis=0)
    rank = jnp.sum(csum * onehot, axis=1) - 1
    counts = csum[-1]
    padded = ((counts + tb - 1) // tb) * tb
    pends = jnp.cumsum(padded)
    pstarts = pends - padded
    dest = jnp.sum(onehot * pstarts[None, :], axis=1) + rank
    n_blocks = (2 * n) // tb + N_EXPERTS
    block_start = jnp.arange(n_blocks, dtype=I32) * tb
    block_e = jnp.minimum(jnp.searchsorted(pends, block_start, side="right"), N_EXPERTS - 1).astype(I32)
    n_used = (pends[-1] // tb).astype(I32).reshape(1)
    return dest.astype(I32), block_e, n_used, n_blocks * tb


def route_weights(w_group, b_group, w_expert, b_expert):
    d = w_group.shape[0]
    n_pad = LANES - N_GROUPS - N_EXPERTS
    w = jnp.concatenate([w_group, w_expert, jnp.zeros((d, n_pad), F32)], axis=1).astype(BF16)
    b = jnp.concatenate([b_group, b_expert, jnp.zeros((n_pad,), F32)]).reshape(1, LANES)
    return w, b


def hier_moe_ln(x, x_pk, route, w_gate, w_up, w_down, g, b, *, layer, tb=256, tm=256):
    dest, block_e, n_used, rows = routing_tables(route, tb=tb)
    xs = moe_dispatch(x_pk, dest, rows, tm=tm)
    ys = moe_experts(xs, block_e, n_used, w_gate, w_up, w_down, layer=layer, tb=tb)
    return moe_combine_ln(x, route, dest, ys, g, b, tm=tm)


def pool_band_layer(x, w_in, w_pool, pool_scale, rel_bias, w_o, g, b, w_route, b_route, *, batch, seq):
    qscale = jnp.concatenate([jnp.ones((POOL_WIDTH,), F32), jnp.full((ATT_WIDTH,), ATT_HEAD_DIM ** -0.5 * LOG2_E, F32),
                              jnp.ones((2 * ATT_WIDTH,), F32)])
    h = matmul(x, (w_in * qscale).astype(BF16), tm=512, tn=1024, out_dtype=BF16)
    a_out = pool_mixer(h, w_pool.astype(BF16), pool_scale.reshape(1, -1), seq=seq, tm=512)
    b_out = band_attention(h, band_bias_table(rel_bias), batch=batch, seq=seq, tq=min(512, seq))
    w_o = w_o.astype(BF16)
    return proj_residual_ln([a_out, b_out], [w_o[:POOL_WIDTH], w_o[POOL_WIDTH:]], x, g, b, w_route, b_route, tm=256)


def mla_layer(x, w_in, g_q, g_kv, w_uq, w_uk, w_uv, w_o, g, b, w_route, b_route, *, batch, seq):
    d = x.shape[1]
    tabs = rope_tables(seq)
    w_in_pad = jnp.concatenate([w_in, jnp.zeros((d, LANES - ROPE_DIM), F32)], axis=1).astype(BF16)
    cq, ckv, k_rope = mla_in_proj(x, w_in_pad, g_q, g_kv, tabs, seq=seq, tm=512)
    w_uq_h = w_uq.reshape(Q_LORA, MLA_HEADS, NOPE_DIM + ROPE_DIM)
    w_uq_cat = jnp.concatenate([w_uq_h, jnp.zeros((Q_LORA, MLA_HEADS, 2 * LANES - NOPE_DIM - ROPE_DIM), F32)], axis=2)
    w_uq_cat = w_uq_cat.reshape(Q_LORA, MLA_HEADS * 2 * LANES).astype(BF16)
    q_cat = mla_q_proj(cq, w_uq_cat, tabs, seq=seq, tm=512, heads_per_step=4)
    kv = matmul(ckv, jnp.concatenate([w_uk, w_uv], axis=1).astype(BF16), tm=512, tn=1024, out_dtype=BF16)
    o = mla_attention(q_cat, kv, k_rope, batch=batch, seq=seq, tq=512, tk=512, heads=2)
    return proj_residual_ln([o], [w_o.astype(BF16)], x, g, b, w_route, b_route, tm=256)


def kernel(x, ab_w_in, ab_w_pool, ab_pool_scale, ab_rel_bias, ab_w_o, mla_w_in, mla_g_q, mla_g_kv, mla_w_uq, mla_w_uk, mla_w_uv, mla_w_o, ln1_g, ln1_b, moe_w_group, moe_b_group, moe_w_expert, moe_b_expert, moe_w_gate, moe_w_up, moe_w_down, ln2_g, ln2_b):
    batch, seq, d = x.shape
    x = x.reshape(batch * seq, d)
    for i in range(DEPTH):
        j = i // 2
        w_route, b_route = route_weights(moe_w_group[i], moe_b_group[i], moe_w_expert[i], moe_b_expert[i])
        if i % 2 == 0:
            x, x_pk, route = pool_band_layer(x, ab_w_in[j], ab_w_pool[j], ab_pool_scale[j], ab_rel_bias[j], ab_w_o[j],
                                             ln1_g[i], ln1_b[i], w_route, b_route, batch=batch, seq=seq)
        else:
            x, x_pk, route = mla_layer(x, mla_w_in[j], mla_g_q[j], mla_g_kv[j], mla_w_uq[j], mla_w_uk[j], mla_w_uv[j],
                                       mla_w_o[j], ln1_g[i], ln1_b[i], w_route, b_route, batch=batch, seq=seq)
        x = hier_moe_ln(x, x_pk, route, moe_w_gate, moe_w_up, moe_w_down, ln2_g[i], ln2_b[i], layer=i)
    return x.reshape(batch, seq, d)
```

```python
import functools

import jax
import jax.numpy as jnp
from jax import lax
from jax.experimental import pallas as pl
from jax.experimental.pallas import tpu as pltpu

F32 = jnp.float32
BF16 = jnp.bfloat16
I32 = jnp.int32
U32 = jnp.uint32

D_MODEL = 2048
DEPTH = 4
CHUNK = 64
CHUNK_SHIFT = 6
GROUP_SHIFT = 3
POOL_WINDOWS = (2, 4, 8, 16)
POOL_GROUP_DIM = 256
POOL_WIDTH = 1024
ATT_WIDTH = 1024
ATT_HEADS = 16
ATT_HEAD_DIM = 64
LEFT_CHUNKS = 8
REL_MAX = 128
MLA_HEADS = 16
Q_LORA = 512
KV_LORA = 512
NOPE_DIM = 128
ROPE_DIM = 64
V_DIM = 128
ROPE_THETA = 10000.0
N_GROUPS = 8
EXPERTS_PER_GROUP = 8
N_EXPERTS = 64
EXPERT_HIDDEN = 512
DEEPNORM_ALPHA = (2 * DEPTH) ** 0.25
LN_EPS = 1e-5
RMS_EPS = 1e-6

LANES = 128
PACK_ROWS = 8
POOL_HALO = 16
BAND_SUB = 128
BAND_KEYS = BAND_SUB + LEFT_CHUNKS * CHUNK
NEG_INF = float("-inf")
LOG2_E = 1.4426950408889634
VMEM_LIMIT = 56 * 1024 * 1024


def _cparams(sem, vmem=VMEM_LIMIT):
    return pltpu.CompilerParams(dimension_semantics=sem, vmem_limit_bytes=vmem)


def _mm_kernel(x_ref, w_ref, o_ref):
    o_ref[...] = jnp.dot(x_ref[...].astype(BF16), w_ref[...], preferred_element_type=F32).astype(o_ref.dtype)


def matmul(x, w, *, tm, tn, out_dtype):
    m, k = x.shape
    n = w.shape[1]
    return pl.pallas_call(
        _mm_kernel,
        grid=(m // tm, n // tn),
        in_specs=[pl.BlockSpec((tm, k), lambda i, j: (i, 0)), pl.BlockSpec((k, tn), lambda i, j: (0, j))],
        out_specs=pl.BlockSpec((tm, tn), lambda i, j: (i, j)),
        out_shape=jax.ShapeDtypeStruct((m, n), out_dtype),
        compiler_params=_cparams(("parallel", "parallel")),
        name="matmul",
    )(x, w)


def _pool_kernel(u_ref, halo_ref, w_ref, scale_ref, o_ref, *, tm, seq):
    i = pl.program_id(0)
    pos0 = (i * tm) % seq
    u = u_ref[...].astype(F32)
    halo = jnp.where(pos0 == 0, 0.0, halo_ref[...].astype(F32))
    ue = jnp.concatenate([halo, u], axis=0)
    pos = pos0 + lax.broadcasted_iota(I32, (tm, 1), 0)
    for g, win in enumerate(POOL_WINDOWS):
        cols = slice(g * POOL_GROUP_DIM, (g + 1) * POOL_GROUP_DIM)
        s = ue[:, cols]
        shift = 1
        while shift < win:
            s = s + pltpu.roll(s, shift, axis=0)
            shift *= 2
        wsum = s[POOL_HALO:, :]
        count = jnp.minimum(pos + 1, win).astype(F32)
        d = wsum / count - u[:, cols]
        y = jnp.dot(d.astype(BF16), w_ref[g], preferred_element_type=F32) * scale_ref[:, cols]
        o_ref[:, cols] = y.astype(o_ref.dtype)


def pool_mixer(h, w_pool, pool_scale, *, seq, tm):
    n = h.shape[0]
    hb = tm // POOL_HALO
    return pl.pallas_call(
        functools.partial(_pool_kernel, tm=tm, seq=seq),
        grid=(n // tm,),
        in_specs=[
            pl.BlockSpec((tm, POOL_WIDTH), lambda i: (i, 0)),
            pl.BlockSpec((POOL_HALO, POOL_WIDTH), lambda i: (jnp.maximum(i * hb - 1, 0), 0)),
            pl.BlockSpec((len(POOL_WINDOWS), POOL_GROUP_DIM, POOL_GROUP_DIM), lambda i: (0, 0, 0)),
            pl.BlockSpec((1, POOL_WIDTH), lambda i: (0, 0)),
        ],
        out_specs=pl.BlockSpec((tm, POOL_WIDTH), lambda i: (i, 0)),
        out_shape=jax.ShapeDtypeStruct((n, POOL_WIDTH), BF16),
        compiler_params=_cparams(("parallel",)),
        name="pool_mixer",
    )(h, h, w_pool, pool_scale)


def _band_kernel(q_ref, kp_ref, kc_ref, vp_ref, vc_ref, bias_ref, o_ref, *, tq):
    i = pl.program_id(1)
    kk = jnp.concatenate([kp_ref[...], kc_ref[...]], axis=0)
    vv = jnp.concatenate([vp_ref[...], vc_ref[...]], axis=0)
    lane = lax.broadcasted_iota(I32, (1, LANES), 1)
    first_head = lane < ATT_HEAD_DIM
    col = lax.broadcasted_iota(I32, (1, BAND_KEYS), 1)
    left = LEFT_CHUNKS * CHUNK
    ones = jnp.ones((BAND_KEYS, LANES), BF16)
    for j in range(tq // BAND_SUB):
        qs = q_ref[j * BAND_SUB:(j + 1) * BAND_SUB, :]
        off = tq - left + j * BAND_SUB
        kw = kk[off:off + BAND_KEYS, :]
        vw = jnp.concatenate([vv[off:off + BAND_KEYS, :], ones], axis=1)
        valid = (i * tq + j * BAND_SUB - left + col) >= 0
        outs = []
        for hh in range(2):
            qm = jnp.where(first_head if hh == 0 else jnp.logical_not(first_head), qs, jnp.zeros_like(qs))
            s = lax.dot_general(qm, kw, (((1,), (1,)), ((), ())), preferred_element_type=F32)
            s = jnp.where(valid, s + bias_ref[hh], NEG_INF)
            m = jnp.max(s, axis=1, keepdims=True)
            p = jnp.exp2(s - m)
            o = jnp.dot(p.astype(BF16), vw, preferred_element_type=F32)
            outs.append(o[:, :LANES] / o[:, LANES:])
        o_ref[j * BAND_SUB:(j + 1) * BAND_SUB, :] = jnp.where(first_head, outs[0], outs[1]).astype(o_ref.dtype)


def band_bias_table(rel_bias):
    heads, rel_size = rel_bias.shape
    r = jnp.arange(BAND_SUB, dtype=I32)[:, None]
    c = jnp.arange(BAND_KEYS, dtype=I32)[None, :]
    qc, kc = r // CHUNK, c // CHUNK
    allowed = (kc >= qc) & (kc <= qc + LEFT_CHUNKS)
    n_lo = BAND_SUB - CHUNK
    n_hi = BAND_KEYS - 1 - REL_MAX
    by_dist = jnp.concatenate([jnp.repeat(rel_bias[:, :1], n_lo, axis=1), rel_bias,
                               jnp.repeat(rel_bias[:, -1:], n_hi, axis=1)], axis=1).astype(F32)
    period = BAND_SUB + BAND_KEYS - 1
    w = jnp.concatenate([by_dist[:, :BAND_KEYS][:, ::-1], by_dist[:, BAND_KEYS:][:, ::-1]], axis=1)
    rows = jnp.tile(w, (1, BAND_SUB))[:, :BAND_SUB * (period - 1)].reshape(heads, BAND_SUB, period - 1)
    return jnp.where(allowed[None], rows[:, :, :BAND_KEYS] * LOG2_E, NEG_INF)


def band_attention(h, bias_tab, *, batch, seq, tq):
    n = h.shape[0]
    nq = seq // tq
    qcol = POOL_WIDTH // LANES
    kcol = qcol + ATT_WIDTH // LANES
    vcol = kcol + ATT_WIDTH // LANES
    cur = lambda c0: (lambda b, i, p: (b * nq + i, c0 + p))
    prev = lambda c0: (lambda b, i, p: (b * nq + jnp.maximum(i - 1, 0), c0 + p))
    blk = (tq, LANES)
    return pl.pallas_call(
        functools.partial(_band_kernel, tq=tq),
        grid=(batch, nq, ATT_HEADS // 2),
        in_specs=[
            pl.BlockSpec(blk, cur(qcol)),
            pl.BlockSpec(blk, prev(kcol)), pl.BlockSpec(blk, cur(kcol)),
            pl.BlockSpec(blk, prev(vcol)), pl.BlockSpec(blk, cur(vcol)),
            pl.BlockSpec((2, BAND_SUB, BAND_KEYS), lambda b, i, p: (p, 0, 0)),
        ],
        out_specs=pl.BlockSpec(blk, lambda b, i, p: (b * nq + i, p)),
        out_shape=jax.ShapeDtypeStruct((n, ATT_WIDTH), BF16),
        compiler_params=_cparams(("parallel", "parallel", "parallel")),
        name="band_attention",
    )(h, h, h, h, h, bias_tab)


def _layer_norm_rows(z, g, b):
    mu = jnp.mean(z, axis=-1, keepdims=True)
    zc = z - mu
    var = jnp.mean(zc * zc, axis=-1, keepdims=True)
    return zc * lax.rsqrt(var + LN_EPS) * g + b


def _pack_tokens(z, ref):
    tm, d = z.shape
    lo = lax.bitcast_convert_type(z[:, :d // 2].astype(BF16).astype(F32), U32)
    hi = lax.bitcast_convert_type(z[:, d // 2:].astype(BF16).astype(F32), U32)
    pk = lax.shift_right_logical(lo, jnp.uint32(16)) | (hi & jnp.uint32(0xFFFF0000))
    for s in range(PACK_ROWS):
        ref[pl.ds(s, tm, stride=PACK_ROWS), :] = pk[:, s * LANES:(s + 1) * LANES]


def _unpack_tokens(ref, first_row, tm):
    pk = jnp.concatenate([ref[pl.ds(first_row + s, tm, stride=PACK_ROWS), :] for s in range(PACK_ROWS)], axis=1)
    lo = lax.bitcast_convert_type(lax.shift_left(pk, jnp.uint32(16)), F32)
    hi = lax.bitcast_convert_type(pk & jnp.uint32(0xFFFF0000), F32)
    return lo, hi


def _proj_ln_kernel(*refs, n_in):
    a_refs, w_refs = refs[:n_in], refs[n_in:2 * n_in]
    x_ref, g_ref, b_ref, wr_ref, br_ref, o_ref, opk_ref, route_ref = refs[2 * n_in:]
    y = jnp.dot(a_refs[0][...], w_refs[0][...], preferred_element_type=F32)
    for a_ref, w_ref in zip(a_refs[1:], w_refs[1:]):
        y = y + jnp.dot(a_ref[...], w_ref[...], preferred_element_type=F32)
    z = DEEPNORM_ALPHA * x_ref[...] + y
    xn = _layer_norm_rows(z, g_ref[...], b_ref[...])
    o_ref[...] = xn
    _pack_tokens(xn, opk_ref)
    route_ref[...] = _route_rows(xn, wr_ref, br_ref)


def proj_residual_ln(acts, weights, x, g, b, w_route, b_route, *, tm):
    n, d = x.shape
    n_in = len(acts)
    const = lambda i: (0, 0)
    in_specs = [pl.BlockSpec((tm, a.shape[1]), lambda i: (i, 0)) for a in acts]
    in_specs += [pl.BlockSpec(w.shape, const) for w in weights]
    in_specs += [pl.BlockSpec((tm, d), lambda i: (i, 0)), pl.BlockSpec((1, d), const), pl.BlockSpec((1, d), const),
                 pl.BlockSpec((d, LANES), const), pl.BlockSpec((1, LANES), const)]
    return pl.pallas_call(
        functools.partial(_proj_ln_kernel, n_in=n_in),
        grid=(n // tm,),
        in_specs=in_specs,
        out_specs=[pl.BlockSpec((tm, d), lambda i: (i, 0)), pl.BlockSpec((tm * PACK_ROWS, LANES), lambda i: (i, 0)),
                   pl.BlockSpec((tm, LANES), lambda i: (i, 0))],
        out_shape=[jax.ShapeDtypeStruct((n, d), F32), jax.ShapeDtypeStruct((n * PACK_ROWS, LANES), U32),
                   jax.ShapeDtypeStruct((n, LANES), F32)],
        compiler_params=_cparams(("parallel",)),
        name="proj_residual_ln",
    )(*acts, *weights, x, g.reshape(1, d), b.reshape(1, d), w_route, b_route)


def _rope_lanes(x, c_ref, s_up_ref, s_dn_ref):
    half = ROPE_DIM // 2
    return (x * c_ref[...] + pltpu.roll(x, half, axis=1) * s_up_ref[...]
            + pltpu.roll(x, LANES - half, axis=1) * s_dn_ref[...])


def _rms_rows(x, g):
    return x * lax.rsqrt(jnp.mean(x * x, axis=-1, keepdims=True) + RMS_EPS) * g


def _mla_in_kernel(x_ref, w_ref, gq_ref, gkv_ref, c_ref, su_ref, sd_ref, cq_ref, ckv_ref, kr_ref):
    h = jnp.dot(x_ref[...].astype(BF16), w_ref[...], preferred_element_type=F32)
    cq_ref[...] = _rms_rows(h[:, :Q_LORA], gq_ref[...]).astype(cq_ref.dtype)
    ckv_ref[...] = _rms_rows(h[:, Q_LORA:Q_LORA + KV_LORA], gkv_ref[...]).astype(ckv_ref.dtype)
    kr_ref[...] = _rope_lanes(h[:, Q_LORA + KV_LORA:], c_ref, su_ref, sd_ref).astype(kr_ref.dtype)


def mla_in_proj(x, w_in_pad, g_q, g_kv, rope_tabs, *, seq, tm):
    n, d = x.shape
    wn = w_in_pad.shape[1]
    npos = seq // tm
    tab = pl.BlockSpec((tm, LANES), lambda i: (i % npos, 0))
    return pl.pallas_call(
        _mla_in_kernel,
        grid=(n // tm,),
        in_specs=[pl.BlockSpec((tm, d), lambda i: (i, 0)), pl.BlockSpec((d, wn), lambda i: (0, 0)),
                  pl.BlockSpec((1, Q_LORA), lambda i: (0, 0)), pl.BlockSpec((1, KV_LORA), lambda i: (0, 0)),
                  tab, tab, tab],
        out_specs=[pl.BlockSpec((tm, Q_LORA), lambda i: (i, 0)), pl.BlockSpec((tm, KV_LORA), lambda i: (i, 0)),
                   pl.BlockSpec((tm, LANES), lambda i: (i, 0))],
        out_shape=[jax.ShapeDtypeStruct((n, Q_LORA), BF16), jax.ShapeDtypeStruct((n, KV_LORA), BF16),
                   jax.ShapeDtypeStruct((n, LANES), BF16)],
        compiler_params=_cparams(("parallel",)),
        name="mla_in_proj",
    )(x, w_in_pad, g_q.reshape(1, -1), g_kv.reshape(1, -1), *rope_tabs)


def _mla_q_kernel(cq_ref, w_ref, c_ref, su_ref, sd_ref, o_ref, *, heads, scale):
    q = jnp.dot(cq_ref[...], w_ref[...], preferred_element_type=F32)
    for hh in range(heads):
        base = hh * 2 * LANES
        o_ref[:, base:base + LANES] = (q[:, base:base + LANES] * scale).astype(o_ref.dtype)
        rot = _rope_lanes(q[:, base + LANES:base + 2 * LANES], c_ref, su_ref, sd_ref)
        o_ref[:, base + LANES:base + 2 * LANES] = (rot * scale).astype(o_ref.dtype)


def mla_q_proj(cq, w_uq_cat, rope_tabs, *, seq, tm, heads_per_step):
    n = cq.shape[0]
    wn = w_uq_cat.shape[1]
    tn = heads_per_step * 2 * LANES
    npos = seq // tm
    tab = pl.BlockSpec((tm, LANES), lambda i, j: (i % npos, 0))
    scale = (NOPE_DIM + ROPE_DIM) ** -0.5 * LOG2_E
    return pl.pallas_call(
        functools.partial(_mla_q_kernel, heads=heads_per_step, scale=scale),
        grid=(n // tm, wn // tn),
        in_specs=[pl.BlockSpec((tm, Q_LORA), lambda i, j: (i, 0)), pl.BlockSpec((Q_LORA, tn), lambda i, j: (0, j)),
                  tab, tab, tab],
        out_specs=pl.BlockSpec((tm, tn), lambda i, j: (i, j)),
        out_shape=jax.ShapeDtypeStruct((n, wn), BF16),
        compiler_params=_cparams(("parallel", "parallel")),
        name="mla_q_proj",
    )(cq, w_uq_cat, *rope_tabs)


def _mla_attn_kernel(q_ref, kn_ref, kr_ref, v_ref, o_ref, m_sc, acc_sc, *, tq, tk, heads):
    qi = pl.program_id(2)
    n_full = qi * (tq // tk)
    m_sc[...] = jnp.full_like(m_sc, NEG_INF)
    acc_sc[...] = jnp.zeros_like(acc_sc)
    ones = jnp.ones((tk, LANES), BF16)
    row_chunk = jnp.right_shift(lax.broadcasted_iota(I32, (tq, 1), 0), CHUNK_SHIFT)
    col_chunk = jnp.right_shift(lax.broadcasted_iota(I32, (1, tk), 1), CHUNK_SHIFT)

    def step(kj, diag):
        off = pl.multiple_of(kj * tk, tk)
        kr = kr_ref[pl.ds(off, tk), :]
        for hh in range(heads):
            k = jnp.concatenate([kn_ref[pl.ds(off, tk), hh * NOPE_DIM:(hh + 1) * NOPE_DIM], kr], axis=1)
            s = lax.dot_general(q_ref[:, hh * 2 * LANES:(hh + 1) * 2 * LANES], k, (((1,), (1,)), ((), ())),
                                preferred_element_type=F32)
            if diag is not None:
                s = jnp.where(col_chunk + diag * (tk // CHUNK) <= row_chunk, s, NEG_INF)
            m_prev = m_sc[hh]
            m_new = jnp.maximum(m_prev, jnp.max(s, axis=1, keepdims=True))
            alpha = jnp.exp2(m_prev - m_new)
            p = jnp.exp2(s - jnp.tile(m_new, (1, tk // LANES)))
            va = jnp.concatenate([v_ref[pl.ds(off, tk), hh * V_DIM:(hh + 1) * V_DIM], ones], axis=1)
            pv = jnp.dot(p.astype(BF16), va, preferred_element_type=F32)
            acc_sc[hh] = jnp.tile(alpha, (1, 2)) * acc_sc[hh] + pv
            m_sc[hh] = m_new

    def full_step(kj, c):
        step(kj, None)
        return c

    lax.fori_loop(0, n_full, full_step, 0)
    for dj in range(tq // tk):
        step(n_full + dj, dj)
    for hh in range(heads):
        acc = acc_sc[hh]
        o_ref[:, hh * V_DIM:(hh + 1) * V_DIM] = (acc[:, :V_DIM] / acc[:, V_DIM:]).astype(o_ref.dtype)


def mla_attention(q_cat, kv, k_rope, *, batch, seq, tq, tk, heads):
    n = q_cat.shape[0]
    nq = seq // tq
    n_hb = MLA_HEADS // heads
    return pl.pallas_call(
        functools.partial(_mla_attn_kernel, tq=tq, tk=tk, heads=heads),
        grid=(batch, n_hb, nq),
        in_specs=[
            pl.BlockSpec((tq, heads * 2 * LANES), lambda b, h, qi: (b * nq + qi, h)),
            pl.BlockSpec((seq, heads * NOPE_DIM), lambda b, h, qi: (b, h), pipeline_mode=pl.Buffered(1)),
            pl.BlockSpec((seq, LANES), lambda b, h, qi: (b, 0), pipeline_mode=pl.Buffered(1)),
            pl.BlockSpec((seq, heads * V_DIM), lambda b, h, qi: (b, n_hb + h), pipeline_mode=pl.Buffered(1)),
        ],
        out_specs=pl.BlockSpec((tq, heads * V_DIM), lambda b, h, qi: (b * nq + qi, h)),
        out_shape=jax.ShapeDtypeStruct((n, MLA_HEADS * V_DIM), BF16),
        scratch_shapes=[pltpu.VMEM((heads, tq, LANES), F32), pltpu.VMEM((heads, tq, 2 * LANES), F32)],
        compiler_params=_cparams(("parallel", "parallel", "arbitrary")),
        name="mla_attention",
    )(q_cat, kv, k_rope, kv)


def rope_tables(seq):
    half = ROPE_DIM // 2
    freq = ROPE_THETA ** (-jnp.arange(half, dtype=F32) / half)
    ang = jnp.arange(seq, dtype=F32)[:, None] * freq[None, :]
    cos, sin = jnp.cos(ang), jnp.sin(ang)
    z = jnp.zeros_like(cos)
    c = jnp.concatenate([cos, cos, z, z], axis=1)
    s_up = jnp.concatenate([z, sin, z, z], axis=1)
    s_dn = jnp.concatenate([-sin, z, z, z], axis=1)
    return c, s_up, s_dn


def _route_rows(xn, w_ref, b_ref):
    logits = jnp.dot(xn.astype(BF16), w_ref[...], preferred_element_type=F32) + b_ref[...]
    tm = logits.shape[0]
    lane = lax.broadcasted_iota(I32, (tm, LANES), 1)
    is_group = lane < N_GROUPS
    lg = jnp.where(is_group, logits, NEG_INF)
    mg = jnp.max(lg, axis=1, keepdims=True)
    g_sel = jnp.min(jnp.where(lg == mg, lane, LANES), axis=1, keepdims=True)
    p_group = 1.0 / jnp.sum(jnp.where(is_group, jnp.exp(lg - mg), 0.0), axis=1, keepdims=True)
    e_lane = lane - N_GROUPS
    in_sel = (e_lane >= 0) & (e_lane < N_EXPERTS) & (jnp.right_shift(e_lane, GROUP_SHIFT) == g_sel)
    le = jnp.where(in_sel, logits, NEG_INF)
    v1 = jnp.max(le, axis=1, keepdims=True)
    i1 = jnp.min(jnp.where(le == v1, lane, LANES), axis=1, keepdims=True)
    le2 = jnp.where(lane == i1, NEG_INF, le)
    v2 = jnp.max(le2, axis=1, keepdims=True)
    i2 = jnp.min(jnp.where(le2 == v2, lane, LANES), axis=1, keepdims=True)
    t = jnp.exp(v2 - v1)
    p1 = 1.0 / (1.0 + t)
    p2 = t / (1.0 + t)
    out = jnp.where(lane == 0, (i1 - N_GROUPS).astype(F32), 0.0)
    out = jnp.where(lane == 1, (i2 - N_GROUPS).astype(F32), out)
    out = jnp.where(lane == 2, p_group * p1, out)
    out = jnp.where(lane == 3, p_group * p2, out)
    return out


def _token_copy(src, src_tok, dst, dst_tok, sem):
    def first_row(tok):
        row = tok * PACK_ROWS
        return row if isinstance(row, int) else pl.multiple_of(row, PACK_ROWS)

    return pltpu.make_async_copy(src.at[pl.ds(first_row(src_tok), PACK_ROWS)],
                                 dst.at[pl.ds(first_row(dst_tok), PACK_ROWS)], sem)


def _dispatch_kernel(dest_ref, x_ref, init_ref, xs_ref, sem, *, tm):
    del init_ref
    base = pl.program_id(0) * tm * 2

    copies = [_token_copy(x_ref, a // 2, xs_ref, dest_ref[base + a], sem) for a in range(2 * tm)]
    for a, cp in enumerate(copies):
        cp.start(priority=a % 2)
    for cp in copies:
        cp.wait()


def moe_dispatch(x_pk, dest, rows, *, tm):
    n = x_pk.shape[0] // PACK_ROWS
    return pl.pallas_call(
        functools.partial(_dispatch_kernel, tm=tm),
        grid_spec=pltpu.PrefetchScalarGridSpec(
            num_scalar_prefetch=1,
            grid=(n // tm,),
            in_specs=[pl.BlockSpec((tm * PACK_ROWS, LANES), lambda i, dest: (i, 0)), pl.BlockSpec(memory_space=pl.ANY)],
            out_specs=pl.BlockSpec(memory_space=pl.ANY),
            scratch_shapes=[pltpu.SemaphoreType.DMA(())],
        ),
        out_shape=jax.ShapeDtypeStruct((rows * PACK_ROWS, LANES), U32),
        input_output_aliases={2: 0},
        compiler_params=_cparams(("arbitrary",)),
        name="moe_dispatch",
    )(dest, x_pk, jnp.zeros((rows * PACK_ROWS, LANES), U32))


def _expert_kernel(b0_ref, nb_ref, nu_ref, xs_ref, wg_ref, wu_ref, wd_ref, ys_ref,
                   xbuf, ybuf, wg_sc, wu_sc, wd_sc, sem_in, sem_out, *, tb):
    e = pl.program_id(0)
    b0, nb, n_used = b0_ref[e], nb_ref[e], nu_ref[0]
    rows = tb * PACK_ROWS

    def in_copy(g):
        src = xs_ref.at[pl.ds(pl.multiple_of(g * rows, rows), rows)]
        return pltpu.make_async_copy(src, xbuf.at[g % 2], sem_in.at[g % 2])

    def out_copy(g):
        dst = ys_ref.at[pl.ds(pl.multiple_of(g * rows, rows), rows)]
        return pltpu.make_async_copy(ybuf.at[g % 2], dst, sem_out.at[g % 2])

    @pl.when(e == 0)
    def _():
        in_copy(0).start()

    @pl.when(nb > 0)
    def _():
        wg_sc[...] = wg_ref[...].astype(BF16)
        wu_sc[...] = wu_ref[...].astype(BF16)
        wd_sc[...] = wd_ref[...].astype(BF16)

        def block(g, c):
            @pl.when(g + 1 < n_used)
            def _():
                in_copy(g + 1).start()

            in_copy(g).wait()

            @pl.when(g >= 2)
            def _():
                out_copy(g - 2).wait()

            lo, hi = _unpack_tokens(xbuf.at[g % 2], 0, tb)
            xb = jnp.concatenate([lo.astype(BF16), hi.astype(BF16)], axis=1)
            hg = jnp.dot(xb, wg_sc[...], preferred_element_type=F32)
            hu = jnp.dot(xb, wu_sc[...], preferred_element_type=F32)
            hb = (hg * jax.nn.sigmoid(hg) * hu).astype(BF16)
            _pack_tokens(jnp.dot(hb, wd_sc[...], preferred_element_type=F32), ybuf.at[g % 2])
            out_copy(g).start()
            return c

        lax.fori_loop(b0, b0 + nb, block, 0)

    @pl.when(e == pl.num_programs(0) - 1)
    def _():
        @pl.when(n_used >= 2)
        def _():
            out_copy(n_used - 2).wait()

        out_copy(n_used - 1).wait()


def moe_experts(xs, first_block, n_blocks, n_used, w_gate, w_up, w_down, *, layer, tb):
    _, n_experts, d, hdim = w_gate.shape
    wmap = lambda e, b0, nb, nu: (layer, e, 0, 0)
    ring = pltpu.VMEM((2, tb * PACK_ROWS, LANES), U32)
    return pl.pallas_call(
        functools.partial(_expert_kernel, tb=tb),
        grid_spec=pltpu.PrefetchScalarGridSpec(
            num_scalar_prefetch=3,
            grid=(n_experts,),
            in_specs=[pl.BlockSpec(memory_space=pl.ANY),
                      pl.BlockSpec((None, None, d, hdim), wmap), pl.BlockSpec((None, None, d, hdim), wmap),
                      pl.BlockSpec((None, None, hdim, d), wmap)],
            out_specs=pl.BlockSpec(memory_space=pl.ANY),
            scratch_shapes=[ring, ring, pltpu.VMEM((d, hdim), BF16), pltpu.VMEM((d, hdim), BF16),
                            pltpu.VMEM((hdim, d), BF16), pltpu.SemaphoreType.DMA((2,)), pltpu.SemaphoreType.DMA((2,))],
        ),
        out_shape=jax.ShapeDtypeStruct(xs.shape, xs.dtype),
        input_output_aliases={3: 0},
        compiler_params=_cparams(("arbitrary",)),
        name="moe_experts",
    )(first_block, n_blocks, n_used, xs, w_gate, w_up, w_down)


def _combine_ln_kernel(dest_ref, x_ref, route_ref, g_ref, b_ref, ys_ref, o_ref, buf, sem, *, tm):
    base = pl.program_id(0) * tm * 2

    copies = [_token_copy(ys_ref, dest_ref[base + a], buf, (a % 2) * tm + a // 2, sem) for a in range(2 * tm)]
    for a, cp in enumerate(copies):
        cp.start(priority=a % 2)
    for cp in copies:
        cp.wait()
    g1 = route_ref[:, 2:3]
    g2 = route_ref[:, 3:4]
    lo1, hi1 = _unpack_tokens(buf, 0, tm)
    lo2, hi2 = _unpack_tokens(buf, tm * PACK_ROWS, tm)
    f = jnp.concatenate([lo1 * g1 + lo2 * g2, hi1 * g1 + hi2 * g2], axis=1)
    z = DEEPNORM_ALPHA * x_ref[...] + f
    o_ref[...] = _layer_norm_rows(z, g_ref[...], b_ref[...])


def moe_combine_ln(x, route, dest, ys, g, b, *, tm):
    n, d = x.shape
    return pl.pallas_call(
        functools.partial(_combine_ln_kernel, tm=tm),
        grid_spec=pltpu.PrefetchScalarGridSpec(
            num_scalar_prefetch=1,
            grid=(n // tm,),
            in_specs=[pl.BlockSpec((tm, d), lambda i, dest: (i, 0)), pl.BlockSpec((tm, LANES), lambda i, dest: (i, 0)),
                      pl.BlockSpec((1, d), lambda i, dest: (0, 0)), pl.BlockSpec((1, d), lambda i, dest: (0, 0)),
                      pl.BlockSpec(memory_space=pl.ANY)],
            out_specs=pl.BlockSpec((tm, d), lambda i, dest: (i, 0)),
            scratch_shapes=[pltpu.VMEM((2 * tm * PACK_ROWS, LANES), U32), pltpu.SemaphoreType.DMA(())],
        ),
        out_shape=jax.ShapeDtypeStruct((n, d), F32),
        compiler_params=_cparams(("arbitrary",)),
        name="moe_combine_ln",
    )(dest, x, route, g.reshape(1, d), b.reshape(1, d), ys)


def routing_tables(route, *, tb):
    n = route.shape[0]
    e_flat = route[:, :2].astype(I32).reshape(-1)
    onehot = (e_flat[:, None] == jnp.arange(N_EXPERTS, dtype=I32)[None, :]).astype(I32)
    csum = jnp.cumsum(onehot, axis=0)
    rank = jnp.sum(csum * onehot, axis=1) - 1
    counts = csum[-1]
    padded = ((counts + tb - 1) // tb) * tb
    pends = jnp.cumsum(padded)
    pstarts = pends - padded
    dest = jnp.sum(onehot * pstarts[None, :], axis=1) + rank
    max_blocks = (2 * n) // tb + N_EXPERTS
    n_used = (pends[-1] // tb).astype(I32).reshape(1)
    return dest.astype(I32), (pstarts // tb).astype(I32), (padded // tb).astype(I32), n_used, max_blocks * tb


def route_weights(w_group, b_group, w_expert, b_expert):
    d = w_group.shape[0]
    n_pad = LANES - N_GROUPS - N_EXPERTS
    w = jnp.concatenate([w_group, w_expert, jnp.zeros((d, n_pad), F32)], axis=1).astype(BF16)
    b = jnp.concatenate([b_group, b_expert, jnp.zeros((n_pad,), F32)]).reshape(1, LANES)
    return w, b


def hier_moe_ln(x, x_pk, route, w_gate, w_up, w_down, g, b, *, layer, tb=256, tm=256):
    dest, first_block, n_blocks, n_used, rows = routing_tables(route, tb=tb)
    xs = moe_dispatch(x_pk, dest, rows, tm=tm)
    ys = moe_experts(xs, first_block, n_blocks, n_used, w_gate, w_up, w_down, layer=layer, tb=tb)
    return moe_combine_ln(x, route, dest, ys, g, b, tm=tm)


def pool_band_layer(x, w_in, w_pool, pool_scale, rel_bias, w_o, g, b, w_route, b_route, *, batch, seq):
    qscale = jnp.concatenate([jnp.ones((POOL_WIDTH,), F32), jnp.full((ATT_WIDTH,), ATT_HEAD_DIM ** -0.5 * LOG2_E, F32),
                              jnp.ones((2 * ATT_WIDTH,), F32)])
    h = matmul(x, (w_in * qscale).astype(BF16), tm=512, tn=1024, out_dtype=BF16)
    a_out = pool_mixer(h, w_pool.astype(BF16), pool_scale.reshape(1, -1), seq=seq, tm=512)
    b_out = band_attention(h, band_bias_table(rel_bias), batch=batch, seq=seq, tq=min(512, seq))
    w_o = w_o.astype(BF16)
    return proj_residual_ln([a_out, b_out], [w_o[:POOL_WIDTH], w_o[POOL_WIDTH:]], x, g, b, w_route, b_route, tm=256)


def mla_layer(x, w_in, g_q, g_kv, w_uq, w_uk, w_uv, w_o, g, b, w_route, b_route, *, batch, seq):
    d = x.shape[1]
    tabs = rope_tables(seq)
    w_in_pad = jnp.concatenate([w_in, jnp.zeros((d, LANES - ROPE_DIM), F32)], axis=1).astype(BF16)
    cq, ckv, k_rope = mla_in_proj(x, w_in_pad, g_q, g_kv, tabs, seq=seq, tm=512)
    w_uq_h = w_uq.reshape(Q_LORA, MLA_HEADS, NOPE_DIM + ROPE_DIM)
    w_uq_cat = jnp.concatenate([w_uq_h, jnp.zeros((Q_LORA, MLA_HEADS, 2 * LANES - NOPE_DIM - ROPE_DIM), F32)], axis=2)
    w_uq_cat = w_uq_cat.reshape(Q_LORA, MLA_HEADS * 2 * LANES).astype(BF16)
    q_cat = mla_q_proj(cq, w_uq_cat, tabs, seq=seq, tm=512, heads_per_step=4)
    kv = matmul(ckv, jnp.concatenate([w_uk, w_uv], axis=1).astype(BF16), tm=512, tn=1024, out_dtype=BF16)
    o = mla_attention(q_cat, kv, k_rope, batch=batch, seq=seq, tq=512, tk=512, heads=8)
    return proj_residual_ln([o], [w_o.astype(BF16)], x, g, b, w_route, b_route, tm=256)


def kernel(x, ab_w_in, ab_w_pool, ab_pool_scale, ab_rel_bias, ab_w_o, mla_w_in, mla_g_q, mla_g_kv, mla_w_uq, mla_w_uk, mla_w_uv, mla_w_o, ln1_g, ln1_b, moe_w_group, moe_b_group, moe_w_expert, moe_b_expert, moe_w_gate, moe_w_up, moe_w_down, ln2_g, ln2_b):
    batch, seq, d = x.shape
    x = x.reshape(batch * seq, d)
    for i in range(DEPTH):
        j = i // 2
        w_route, b_route = route_weights(moe_w_group[i], moe_b_group[i], moe_w_expert[i], moe_b_expert[i])
        if i % 2 == 0:
            x, x_pk, route = pool_band_layer(x, ab_w_in[j], ab_w_pool[j], ab_pool_scale[j], ab_rel_bias[j], ab_w_o[j],
                                             ln1_g[i], ln1_b[i], w_route, b_route, batch=batch, seq=seq)
        else:
            x, x_pk, route = mla_layer(x, mla_w_in[j], mla_g_q[j], mla_g_kv[j], mla_w_uq[j], mla_w_uk[j], mla_w_uv[j],
                                       mla_w_o[j], ln1_g[i], ln1_b[i], w_route, b_route, batch=batch, seq=seq)
        x = hier_moe_ln(x, x_pk, route, moe_w_gate, moe_w_up, moe_w_down, ln2_g[i], ln2_b[i], layer=i)
    return x.reshape(batch, seq, d)
```

```python
import functools

import jax
import jax.numpy as jnp
from jax import lax
from jax.experimental import pallas as pl
from jax.experimental.pallas import tpu as pltpu

F32 = jnp.float32
BF16 = jnp.bfloat16
I32 = jnp.int32
U32 = jnp.uint32

D_MODEL = 2048
DEPTH = 4
CHUNK = 64
CHUNK_SHIFT = 6
GROUP_SHIFT = 3
POOL_WINDOWS = (2, 4, 8, 16)
POOL_GROUP_DIM = 256
POOL_WIDTH = 1024
ATT_WIDTH = 1024
ATT_HEADS = 16
ATT_HEAD_DIM = 64
LEFT_CHUNKS = 8
REL_MAX = 128
MLA_HEADS = 16
Q_LORA = 512
KV_LORA = 512
NOPE_DIM = 128
ROPE_DIM = 64
V_DIM = 128
ROPE_THETA = 10000.0
N_GROUPS = 8
EXPERTS_PER_GROUP = 8
N_EXPERTS = 64
EXPERT_HIDDEN = 512
DEEPNORM_ALPHA = (2 * DEPTH) ** 0.25
LN_EPS = 1e-5
RMS_EPS = 1e-6

LANES = 128
PACK_ROWS = D_MODEL // 2 // LANES
PROJ_SPLIT = 2
POOL_HALO = 16
BAND_SUB = 128
BAND_KEYS = BAND_SUB + LEFT_CHUNKS * CHUNK
NEG_INF = float("-inf")
LOG2_E = 1.4426950408889634
VMEM_LIMIT = 56 * 1024 * 1024


def _cparams(sem, vmem=VMEM_LIMIT):
    return pltpu.CompilerParams(dimension_semantics=sem, vmem_limit_bytes=vmem)


def _mm_kernel(x_ref, w_ref, o_ref):
    o_ref[...] = jnp.dot(x_ref[...].astype(BF16), w_ref[...], preferred_element_type=F32).astype(o_ref.dtype)


def matmul(x, w, *, tm, tn, out_dtype):
    m, k = x.shape
    n = w.shape[1]
    return pl.pallas_call(
        _mm_kernel,
        grid=(m // tm, n // tn),
        in_specs=[pl.BlockSpec((tm, k), lambda i, j: (i, 0)), pl.BlockSpec((k, tn), lambda i, j: (0, j))],
        out_specs=pl.BlockSpec((tm, tn), lambda i, j: (i, j)),
        out_shape=jax.ShapeDtypeStruct((m, n), out_dtype),
        compiler_params=_cparams(("parallel", "parallel")),
        name="matmul",
    )(x, w)


def _pool_kernel(u_ref, halo_ref, w_ref, scale_ref, o_ref, *, tm, seq):
    i = pl.program_id(0)
    pos0 = (i * tm) % seq
    u = u_ref[...].astype(F32)
    halo = jnp.where(pos0 == 0, 0.0, halo_ref[...].astype(F32))
    ue = jnp.concatenate([halo, u], axis=0)
    pos = pos0 + lax.broadcasted_iota(I32, (tm, 1), 0)
    for g, win in enumerate(POOL_WINDOWS):
        cols = slice(g * POOL_GROUP_DIM, (g + 1) * POOL_GROUP_DIM)
        s = ue[:, cols]
        shift = 1
        while shift < win:
            s = s + pltpu.roll(s, shift, axis=0)
            shift *= 2
        wsum = s[POOL_HALO:, :]
        count = jnp.minimum(pos + 1, win).astype(F32)
        d = wsum / count - u[:, cols]
        y = jnp.dot(d.astype(BF16), w_ref[g], preferred_element_type=F32) * scale_ref[:, cols]
        o_ref[:, cols] = y.astype(o_ref.dtype)


def pool_mixer(h, w_pool, pool_scale, *, seq, tm):
    n = h.shape[0]
    hb = tm // POOL_HALO
    return pl.pallas_call(
        functools.partial(_pool_kernel, tm=tm, seq=seq),
        grid=(n // tm,),
        in_specs=[
            pl.BlockSpec((tm, POOL_WIDTH), lambda i: (i, 0)),
            pl.BlockSpec((POOL_HALO, POOL_WIDTH), lambda i: (jnp.maximum(i * hb - 1, 0), 0)),
            pl.BlockSpec((len(POOL_WINDOWS), POOL_GROUP_DIM, POOL_GROUP_DIM), lambda i: (0, 0, 0)),
            pl.BlockSpec((1, POOL_WIDTH), lambda i: (0, 0)),
        ],
        out_specs=pl.BlockSpec((tm, POOL_WIDTH), lambda i: (i, 0)),
        out_shape=jax.ShapeDtypeStruct((n, POOL_WIDTH), BF16),
        compiler_params=_cparams(("parallel",)),
        name="pool_mixer",
    )(h, h, w_pool, pool_scale)


def _band_kernel(q_ref, kp_ref, kc_ref, vp_ref, vc_ref, bias_ref, o_ref, *, tq, pairs):
    i = pl.program_id(1)
    lane = lax.broadcasted_iota(I32, (1, LANES), 1)
    first_head = lane < ATT_HEAD_DIM
    col = lax.broadcasted_iota(I32, (1, BAND_KEYS), 1)
    left = LEFT_CHUNKS * CHUNK
    ones = jnp.ones((BAND_KEYS, LANES), BF16)
    for pr in range(pairs):
        lanes = slice(pr * LANES, (pr + 1) * LANES)
        kk = jnp.concatenate([kp_ref[:, lanes], kc_ref[:, lanes]], axis=0)
        vv = jnp.concatenate([vp_ref[:, lanes], vc_ref[:, lanes]], axis=0)
        for j in range(tq // BAND_SUB):
            rows = slice(j * BAND_SUB, (j + 1) * BAND_SUB)
            qs = q_ref[rows, lanes]
            off = tq - left + j * BAND_SUB
            kw = kk[off:off + BAND_KEYS, :]
            vw = jnp.concatenate([vv[off:off + BAND_KEYS, :], ones], axis=1)
            valid = (i * tq + j * BAND_SUB - left + col) >= 0
            outs = []
            for hh in range(2):
                qm = jnp.where(first_head if hh == 0 else jnp.logical_not(first_head), qs, jnp.zeros_like(qs))
                s = lax.dot_general(qm, kw, (((1,), (1,)), ((), ())), preferred_element_type=F32)
                s = jnp.where(valid, s + bias_ref[2 * pr + hh], NEG_INF)
                m = jnp.max(s, axis=1, keepdims=True)
                p = jnp.exp2(s - m)
                o = jnp.dot(p.astype(BF16), vw, preferred_element_type=F32)
                outs.append(o[:, :LANES] / o[:, LANES:])
            o_ref[rows, lanes] = jnp.where(first_head, outs[0], outs[1]).astype(o_ref.dtype)


def band_bias_table(rel_bias):
    heads, rel_size = rel_bias.shape
    r = jnp.arange(BAND_SUB, dtype=I32)[:, None]
    c = jnp.arange(BAND_KEYS, dtype=I32)[None, :]
    qc, kc = r // CHUNK, c // CHUNK
    allowed = (kc >= qc) & (kc <= qc + LEFT_CHUNKS)
    n_lo = BAND_SUB - CHUNK
    n_hi = BAND_KEYS - 1 - REL_MAX
    by_dist = jnp.concatenate([jnp.repeat(rel_bias[:, :1], n_lo, axis=1), rel_bias,
                               jnp.repeat(rel_bias[:, -1:], n_hi, axis=1)], axis=1).astype(F32)
    period = BAND_SUB + BAND_KEYS - 1
    w = jnp.concatenate([by_dist[:, :BAND_KEYS][:, ::-1], by_dist[:, BAND_KEYS:][:, ::-1]], axis=1)
    rows = jnp.tile(w, (1, BAND_SUB))[:, :BAND_SUB * (period - 1)].reshape(heads, BAND_SUB, period - 1)
    return jnp.where(allowed[None], rows[:, :, :BAND_KEYS] * LOG2_E, NEG_INF)


def band_attention(h, bias_tab, *, batch, seq, tq, pairs):
    n = h.shape[0]
    nq = seq // tq
    width = pairs * LANES
    qcol = POOL_WIDTH // width
    kcol = qcol + ATT_WIDTH // width
    vcol = kcol + ATT_WIDTH // width
    cur = lambda c0: (lambda b, i, p: (b * nq + i, c0 + p))
    prev = lambda c0: (lambda b, i, p: (b * nq + jnp.maximum(i - 1, 0), c0 + p))
    blk = (tq, width)
    return pl.pallas_call(
        functools.partial(_band_kernel, tq=tq, pairs=pairs),
        grid=(batch, nq, ATT_HEADS // (2 * pairs)),
        in_specs=[
            pl.BlockSpec(blk, cur(qcol)),
            pl.BlockSpec(blk, prev(kcol)), pl.BlockSpec(blk, cur(kcol)),
            pl.BlockSpec(blk, prev(vcol)), pl.BlockSpec(blk, cur(vcol)),
            pl.BlockSpec((2 * pairs, BAND_SUB, BAND_KEYS), lambda b, i, p: (p, 0, 0)),
        ],
        out_specs=pl.BlockSpec(blk, lambda b, i, p: (b * nq + i, p)),
        out_shape=jax.ShapeDtypeStruct((n, ATT_WIDTH), BF16),
        compiler_params=_cparams(("parallel", "parallel", "parallel")),
        name="band_attention",
    )(h, h, h, h, h, bias_tab)


def _layer_norm_rows(z, g, b):
    mu = jnp.mean(z, axis=-1, keepdims=True)
    zc = z - mu
    var = jnp.mean(zc * zc, axis=-1, keepdims=True)
    return zc * lax.rsqrt(var + LN_EPS) * g + b


def _pack_tokens(z, ref):
    tm, d = z.shape
    lo = lax.bitcast_convert_type(z[:, :d // 2].astype(BF16).astype(F32), U32)
    hi = lax.bitcast_convert_type(z[:, d // 2:].astype(BF16).astype(F32), U32)
    pk = lax.shift_right_logical(lo, jnp.uint32(16)) | (hi & jnp.uint32(0xFFFF0000))
    for s in range(PACK_ROWS):
        ref[pl.ds(s, tm, stride=PACK_ROWS), :] = pk[:, s * LANES:(s + 1) * LANES]


def _unpack_tokens(ref, first_row, tm):
    pk = jnp.concatenate([ref[pl.ds(first_row + s, tm, stride=PACK_ROWS), :] for s in range(PACK_ROWS)], axis=1)
    lo = lax.bitcast_convert_type(lax.shift_left(pk, jnp.uint32(16)), F32)
    hi = lax.bitcast_convert_type(pk & jnp.uint32(0xFFFF0000), F32)
    return lo, hi


def _proj_ln_kernel(*refs, n_in):
    a_refs, w_refs = refs[:n_in], refs[n_in:2 * n_in]
    x_ref, g_ref, b_ref, wr_ref, br_ref, o_ref, opk_ref, route_ref = refs[2 * n_in:]
    tm = x_ref.shape[0]
    sub = tm // PROJ_SPLIT
    for r in range(PROJ_SPLIT):
        rows = pl.ds(r * sub, sub)
        y = jnp.dot(a_refs[0][rows, :], w_refs[0][...], preferred_element_type=F32)
        for a_ref, w_ref in zip(a_refs[1:], w_refs[1:]):
            y = y + jnp.dot(a_ref[rows, :], w_ref[...], preferred_element_type=F32)
        z = DEEPNORM_ALPHA * x_ref[rows, :] + y
        xn = _layer_norm_rows(z, g_ref[...], b_ref[...])
        o_ref[rows, :] = xn
        _pack_tokens(xn, opk_ref.at[pl.ds(r * sub * PACK_ROWS, sub * PACK_ROWS)])
        route_ref[rows, :] = _route_rows(xn, wr_ref, br_ref)


def proj_residual_ln(acts, weights, x, g, b, w_route, b_route, *, tm):
    n, d = x.shape
    n_in = len(acts)
    const = lambda i: (0, 0)
    in_specs = [pl.BlockSpec((tm, a.shape[1]), lambda i: (i, 0)) for a in acts]
    in_specs += [pl.BlockSpec(w.shape, const, pipeline_mode=pl.Buffered(1)) for w in weights]
    in_specs += [pl.BlockSpec((tm, d), lambda i: (i, 0)), pl.BlockSpec((1, d), const), pl.BlockSpec((1, d), const),
                 pl.BlockSpec((d, LANES), const), pl.BlockSpec((1, LANES), const)]
    return pl.pallas_call(
        functools.partial(_proj_ln_kernel, n_in=n_in),
        grid=(n // tm,),
        in_specs=in_specs,
        out_specs=[pl.BlockSpec((tm, d), lambda i: (i, 0)), pl.BlockSpec((tm * PACK_ROWS, LANES), lambda i: (i, 0)),
                   pl.BlockSpec((tm, LANES), lambda i: (i, 0))],
        out_shape=[jax.ShapeDtypeStruct((n, d), F32), jax.ShapeDtypeStruct((n * PACK_ROWS, LANES), U32),
                   jax.ShapeDtypeStruct((n, LANES), F32)],
        compiler_params=_cparams(("parallel",)),
        name="proj_residual_ln",
    )(*acts, *weights, x, g.reshape(1, d), b.reshape(1, d), w_route, b_route)


def _rope_lanes(x, c_ref, s_up_ref, s_dn_ref):
    half = ROPE_DIM // 2
    return (x * c_ref[...] + pltpu.roll(x, half, axis=1) * s_up_ref[...]
            + pltpu.roll(x, LANES - half, axis=1) * s_dn_ref[...])


def _rms_rows(x, g):
    return x * lax.rsqrt(jnp.mean(x * x, axis=-1, keepdims=True) + RMS_EPS) * g


def _mla_in_kernel(x_ref, w_ref, gq_ref, gkv_ref, c_ref, su_ref, sd_ref, cq_ref, ckv_ref, kr_ref):
    h = jnp.dot(x_ref[...].astype(BF16), w_ref[...], preferred_element_type=F32)
    cq_ref[...] = _rms_rows(h[:, :Q_LORA], gq_ref[...]).astype(cq_ref.dtype)
    ckv_ref[...] = _rms_rows(h[:, Q_LORA:Q_LORA + KV_LORA], gkv_ref[...]).astype(ckv_ref.dtype)
    kr_ref[...] = _rope_lanes(h[:, Q_LORA + KV_LORA:], c_ref, su_ref, sd_ref).astype(kr_ref.dtype)


def mla_in_proj(x, w_in_pad, g_q, g_kv, rope_tabs, *, seq, tm):
    n, d = x.shape
    wn = w_in_pad.shape[1]
    npos = seq // tm
    tab = pl.BlockSpec((tm, LANES), lambda i: (i % npos, 0))
    return pl.pallas_call(
        _mla_in_kernel,
        grid=(n // tm,),
        in_specs=[pl.BlockSpec((tm, d), lambda i: (i, 0)), pl.BlockSpec((d, wn), lambda i: (0, 0)),
                  pl.BlockSpec((1, Q_LORA), lambda i: (0, 0)), pl.BlockSpec((1, KV_LORA), lambda i: (0, 0)),
                  tab, tab, tab],
        out_specs=[pl.BlockSpec((tm, Q_LORA), lambda i: (i, 0)), pl.BlockSpec((tm, KV_LORA), lambda i: (i, 0)),
                   pl.BlockSpec((tm, LANES), lambda i: (i, 0))],
        out_shape=[jax.ShapeDtypeStruct((n, Q_LORA), BF16), jax.ShapeDtypeStruct((n, KV_LORA), BF16),
                   jax.ShapeDtypeStruct((n, LANES), BF16)],
        compiler_params=_cparams(("parallel",)),
        name="mla_in_proj",
    )(x, w_in_pad, g_q.reshape(1, -1), g_kv.reshape(1, -1), *rope_tabs)


def _mla_q_kernel(cq_ref, w_ref, c_ref, su_ref, sd_ref, o_ref, *, heads, scale):
    q = jnp.dot(cq_ref[...], w_ref[...], preferred_element_type=F32)
    for hh in range(heads):
        base = hh * 2 * LANES
        o_ref[:, base:base + LANES] = (q[:, base:base + LANES] * scale).astype(o_ref.dtype)
        rot = _rope_lanes(q[:, base + LANES:base + 2 * LANES], c_ref, su_ref, sd_ref)
        o_ref[:, base + LANES:base + 2 * LANES] = (rot * scale).astype(o_ref.dtype)


def mla_q_proj(cq, w_uq_cat, rope_tabs, *, seq, tm, heads_per_step):
    n = cq.shape[0]
    wn = w_uq_cat.shape[1]
    tn = heads_per_step * 2 * LANES
    npos = seq // tm
    tab = pl.BlockSpec((tm, LANES), lambda i, j: (i % npos, 0))
    scale = (NOPE_DIM + ROPE_DIM) ** -0.5 * LOG2_E
    return pl.pallas_call(
        functools.partial(_mla_q_kernel, heads=heads_per_step, scale=scale),
        grid=(n // tm, wn // tn),
        in_specs=[pl.BlockSpec((tm, Q_LORA), lambda i, j: (i, 0)), pl.BlockSpec((Q_LORA, tn), lambda i, j: (0, j)),
                  tab, tab, tab],
        out_specs=pl.BlockSpec((tm, tn), lambda i, j: (i, j)),
        out_shape=jax.ShapeDtypeStruct((n, wn), BF16),
        compiler_params=_cparams(("parallel", "parallel")),
        name="mla_q_proj",
    )(cq, w_uq_cat, *rope_tabs)


def _mla_attn_kernel(q_ref, kn_ref, kr_ref, v_ref, o_ref, m_sc, acc_sc, *, tq, tk, heads):
    qi = pl.program_id(2)
    n_full = qi * (tq // tk)
    m_sc[...] = jnp.full_like(m_sc, NEG_INF)
    acc_sc[...] = jnp.zeros_like(acc_sc)
    ones = jnp.ones((tk, LANES), BF16)
    row_chunk = jnp.right_shift(lax.broadcasted_iota(I32, (tq, 1), 0), CHUNK_SHIFT)
    col_chunk = jnp.right_shift(lax.broadcasted_iota(I32, (1, tk), 1), CHUNK_SHIFT)

    def step(kj, diag):
        off = pl.multiple_of(kj * tk, tk)
        kr = kr_ref[pl.ds(off, tk), :]
        for hh in range(heads):
            k = jnp.concatenate([kn_ref[pl.ds(off, tk), hh * NOPE_DIM:(hh + 1) * NOPE_DIM], kr], axis=1)
            s = lax.dot_general(q_ref[:, hh * 2 * LANES:(hh + 1) * 2 * LANES], k, (((1,), (1,)), ((), ())),
                                preferred_element_type=F32)
            if diag is not None:
                s = jnp.where(col_chunk + diag * (tk // CHUNK) <= row_chunk, s, NEG_INF)
            m_prev = m_sc[hh]
            m_new = jnp.maximum(m_prev, jnp.max(s, axis=1, keepdims=True))
            alpha = jnp.exp2(m_prev - m_new)
            p = jnp.exp2(s - jnp.tile(m_new, (1, tk // LANES)))
            va = jnp.concatenate([v_ref[pl.ds(off, tk), hh * V_DIM:(hh + 1) * V_DIM], ones], axis=1)
            pv = jnp.dot(p.astype(BF16), va, preferred_element_type=F32)
            acc_sc[hh] = jnp.tile(alpha, (1, 2)) * acc_sc[hh] + pv
            m_sc[hh] = m_new

    def two_full_steps(i, c):
        step(2 * i, None)
        step(2 * i + 1, None)
        return c

    lax.fori_loop(0, n_full // 2, two_full_steps, 0)

    @pl.when(n_full % 2 == 1)
    def _():
        step(n_full - 1, None)

    for dj in range(tq // tk):
        step(n_full + dj, dj)
    for hh in range(heads):
        acc = acc_sc[hh]
        o_ref[:, hh * V_DIM:(hh + 1) * V_DIM] = (acc[:, :V_DIM] / acc[:, V_DIM:]).astype(o_ref.dtype)


def mla_attention(q_cat, kv, k_rope, *, batch, seq, tq, tk, heads):
    n = q_cat.shape[0]
    nq = seq // tq
    n_hb = MLA_HEADS // heads
    return pl.pallas_call(
        functools.partial(_mla_attn_kernel, tq=tq, tk=tk, heads=heads),
        grid=(batch, n_hb, nq),
        in_specs=[
            pl.BlockSpec((tq, heads * 2 * LANES), lambda b, h, qi: (b * nq + qi, h)),
            pl.BlockSpec((seq, heads * NOPE_DIM), lambda b, h, qi: (b, h), pipeline_mode=pl.Buffered(1)),
            pl.BlockSpec((seq, LANES), lambda b, h, qi: (b, 0), pipeline_mode=pl.Buffered(1)),
            pl.BlockSpec((seq, heads * V_DIM), lambda b, h, qi: (b, n_hb + h), pipeline_mode=pl.Buffered(1)),
        ],
        out_specs=pl.BlockSpec((tq, heads * V_DIM), lambda b, h, qi: (b * nq + qi, h)),
        out_shape=jax.ShapeDtypeStruct((n, MLA_HEADS * V_DIM), BF16),
        scratch_shapes=[pltpu.VMEM((heads, tq, LANES), F32), pltpu.VMEM((heads, tq, 2 * LANES), F32)],
        compiler_params=_cparams(("parallel", "parallel", "arbitrary")),
        name="mla_attention",
    )(q_cat, kv, k_rope, kv)


def rope_tables(seq):
    half = ROPE_DIM // 2
    freq = ROPE_THETA ** (-jnp.arange(half, dtype=F32) / half)
    ang = jnp.arange(seq, dtype=F32)[:, None] * freq[None, :]
    cos, sin = jnp.cos(ang), jnp.sin(ang)
    z = jnp.zeros_like(cos)
    c = jnp.concatenate([cos, cos, z, z], axis=1)
    s_up = jnp.concatenate([z, sin, z, z], axis=1)
    s_dn = jnp.concatenate([-sin, z, z, z], axis=1)
    return c, s_up, s_dn


def _route_rows(xn, w_ref, b_ref):
    logits = jnp.dot(xn.astype(BF16), w_ref[...], preferred_element_type=F32) + b_ref[...]
    tm = logits.shape[0]
    lane = lax.broadcasted_iota(I32, (tm, LANES), 1)
    is_group = lane < N_GROUPS
    lg = jnp.where(is_group, logits, NEG_INF)
    mg = jnp.max(lg, axis=1, keepdims=True)
    g_sel = jnp.min(jnp.where(lg == mg, lane, LANES), axis=1, keepdims=True)
    p_group = 1.0 / jnp.sum(jnp.where(is_group, jnp.exp(lg - mg), 0.0), axis=1, keepdims=True)
    e_lane = lane - N_GROUPS
    in_sel = (e_lane >= 0) & (e_lane < N_EXPERTS) & (jnp.right_shift(e_lane, GROUP_SHIFT) == g_sel)
    le = jnp.where(in_sel, logits, NEG_INF)
    v1 = jnp.max(le, axis=1, keepdims=True)
    i1 = jnp.min(jnp.where(le == v1, lane, LANES), axis=1, keepdims=True)
    le2 = jnp.where(lane == i1, NEG_INF, le)
    v2 = jnp.max(le2, axis=1, keepdims=True)
    i2 = jnp.min(jnp.where(le2 == v2, lane, LANES), axis=1, keepdims=True)
    t = jnp.exp(v2 - v1)
    p1 = 1.0 / (1.0 + t)
    p2 = t / (1.0 + t)
    out = jnp.where(lane == 0, (i1 - N_GROUPS).astype(F32), 0.0)
    out = jnp.where(lane == 1, (i2 - N_GROUPS).astype(F32), out)
    out = jnp.where(lane == 2, p_group * p1, out)
    out = jnp.where(lane == 3, p_group * p2, out)
    return out


def _token_copy(src, src_tok, dst, dst_tok, sem):
    def first_row(tok):
        row = tok * PACK_ROWS
        return row if isinstance(row, int) else pl.multiple_of(row, PACK_ROWS)

    return pltpu.make_async_copy(src.at[pl.ds(first_row(src_tok), PACK_ROWS)],
                                 dst.at[pl.ds(first_row(dst_tok), PACK_ROWS)], sem)


def _dispatch_kernel(dest_ref, x_ref, init_ref, xs_ref, sem, *, tm):
    del init_ref
    base = pl.program_id(0) * tm * 2

    copies = [_token_copy(x_ref, a // 2, xs_ref, dest_ref[base + a], sem) for a in range(2 * tm)]
    for a, cp in enumerate(copies):
        cp.start(priority=a % 2)
    for cp in copies:
        cp.wait()


def moe_dispatch(x_pk, dest, rows, *, tm):
    n = x_pk.shape[0] // PACK_ROWS
    return pl.pallas_call(
        functools.partial(_dispatch_kernel, tm=tm),
        grid_spec=pltpu.PrefetchScalarGridSpec(
            num_scalar_prefetch=1,
            grid=(n // tm,),
            in_specs=[pl.BlockSpec((tm * PACK_ROWS, LANES), lambda i, dest: (i, 0)), pl.BlockSpec(memory_space=pl.ANY)],
            out_specs=pl.BlockSpec(memory_space=pl.ANY),
            scratch_shapes=[pltpu.SemaphoreType.DMA(())],
        ),
        out_shape=jax.ShapeDtypeStruct((rows * PACK_ROWS, LANES), U32),
        input_output_aliases={2: 0},
        compiler_params=_cparams(("arbitrary",)),
        name="moe_dispatch",
    )(dest, x_pk, jnp.zeros((rows * PACK_ROWS, LANES), U32))


def _expert_kernel(b0_ref, nb_ref, nu_ref, xs_ref, wg_ref, wu_ref, wd_ref, ys_ref,
                   xbuf, ybuf, wg_sc, wu_sc, wd_sc, sem_in, sem_out, *, tb):
    e = pl.program_id(0)
    b0, nb, n_used = b0_ref[e], nb_ref[e], nu_ref[0]
    rows = tb * PACK_ROWS

    def in_copy(g):
        src = xs_ref.at[pl.ds(pl.multiple_of(g * rows, rows), rows)]
        return pltpu.make_async_copy(src, xbuf.at[g % 2], sem_in.at[g % 2])

    def out_copy(g):
        dst = ys_ref.at[pl.ds(pl.multiple_of(g * rows, rows), rows)]
        return pltpu.make_async_copy(ybuf.at[g % 2], dst, sem_out.at[g % 2])

    @pl.when(e == 0)
    def _():
        in_copy(0).start()

    @pl.when(nb > 0)
    def _():
        wg_sc[...] = wg_ref[...].astype(BF16)
        wu_sc[...] = wu_ref[...].astype(BF16)
        wd_sc[...] = wd_ref[...].astype(BF16)

        def block(g, c):
            @pl.when(g + 1 < n_used)
            def _():
                in_copy(g + 1).start()

            in_copy(g).wait()

            @pl.when(g >= 2)
            def _():
                out_copy(g - 2).wait()

            lo, hi = _unpack_tokens(xbuf.at[g % 2], 0, tb)
            xb = jnp.concatenate([lo.astype(BF16), hi.astype(BF16)], axis=1)
            hg = jnp.dot(xb, wg_sc[...], preferred_element_type=F32)
            hu = jnp.dot(xb, wu_sc[...], preferred_element_type=F32)
            hb = (hg * jax.nn.sigmoid(hg) * hu).astype(BF16)
            _pack_tokens(jnp.dot(hb, wd_sc[...], preferred_element_type=F32), ybuf.at[g % 2])
            out_copy(g).start()
            return c

        lax.fori_loop(b0, b0 + nb, block, 0)

    @pl.when(e == pl.num_programs(0) - 1)
    def _():
        @pl.when(n_used >= 2)
        def _():
            out_copy(n_used - 2).wait()

        out_copy(n_used - 1).wait()


def moe_experts(xs, first_block, n_blocks, n_used, w_gate, w_up, w_down, *, layer, tb):
    _, n_experts, d, hdim = w_gate.shape
    wmap = lambda e, b0, nb, nu: (layer, e, 0, 0)
    ring = pltpu.VMEM((2, tb * PACK_ROWS, LANES), U32)
    return pl.pallas_call(
        functools.partial(_expert_kernel, tb=tb),
        grid_spec=pltpu.PrefetchScalarGridSpec(
            num_scalar_prefetch=3,
            grid=(n_experts,),
            in_specs=[pl.BlockSpec(memory_space=pl.ANY),
                      pl.BlockSpec((None, None, d, hdim), wmap), pl.BlockSpec((None, None, d, hdim), wmap),
                      pl.BlockSpec((None, None, hdim, d), wmap)],
            out_specs=pl.BlockSpec(memory_space=pl.ANY),
            scratch_shapes=[ring, ring, pltpu.VMEM((d, hdim), BF16), pltpu.VMEM((d, hdim), BF16),
                            pltpu.VMEM((hdim, d), BF16), pltpu.SemaphoreType.DMA((2,)), pltpu.SemaphoreType.DMA((2,))],
        ),
        out_shape=jax.ShapeDtypeStruct(xs.shape, xs.dtype),
        input_output_aliases={3: 0},
        compiler_params=_cparams(("arbitrary",)),
        name="moe_experts",
    )(first_block, n_blocks, n_used, xs, w_gate, w_up, w_down)


def _combine_ln_kernel(dest_ref, x_ref, route_ref, g_ref, b_ref, ys_ref, o_ref, buf, sem, *, tm):
    i = pl.program_id(0)

    def gathers(step):
        slot = step % 2
        return [_token_copy(ys_ref, dest_ref[step * 2 * tm + a], buf.at[slot], (a % 2) * tm + a // 2, sem.at[slot])
                for a in range(2 * tm)]

    def start(step):
        for a, cp in enumerate(gathers(step)):
            cp.start(priority=a % 2)

    @pl.when(i == 0)
    def _():
        start(0)

    @pl.when(i + 1 < pl.num_programs(0))
    def _():
        start(i + 1)

    for cp in gathers(i):
        cp.wait()
    cur = buf.at[i % 2]
    g1 = route_ref[:, 2:3]
    g2 = route_ref[:, 3:4]
    lo1, hi1 = _unpack_tokens(cur, 0, tm)
    lo2, hi2 = _unpack_tokens(cur, tm * PACK_ROWS, tm)
    f = jnp.concatenate([lo1 * g1 + lo2 * g2, hi1 * g1 + hi2 * g2], axis=1)
    z = DEEPNORM_ALPHA * x_ref[...] + f
    o_ref[...] = _layer_norm_rows(z, g_ref[...], b_ref[...])


def moe_combine_ln(x, route, dest, ys, g, b, *, tm):
    n, d = x.shape
    return pl.pallas_call(
        functools.partial(_combine_ln_kernel, tm=tm),
        grid_spec=pltpu.PrefetchScalarGridSpec(
            num_scalar_prefetch=1,
            grid=(n // tm,),
            in_specs=[pl.BlockSpec((tm, d), lambda i, dest: (i, 0)), pl.BlockSpec((tm, LANES), lambda i, dest: (i, 0)),
                      pl.BlockSpec((1, d), lambda i, dest: (0, 0)), pl.BlockSpec((1, d), lambda i, dest: (0, 0)),
                      pl.BlockSpec(memory_space=pl.ANY)],
            out_specs=pl.BlockSpec((tm, d), lambda i, dest: (i, 0)),
            scratch_shapes=[pltpu.VMEM((2, 2 * tm * PACK_ROWS, LANES), U32), pltpu.SemaphoreType.DMA((2,))],
        ),
        out_shape=jax.ShapeDtypeStruct((n, d), F32),
        compiler_params=_cparams(("arbitrary",)),
        name="moe_combine_ln",
    )(dest, x, route, g.reshape(1, d), b.reshape(1, d), ys)


def routing_tables(route, *, tb):
    n = route.shape[0]
    e_flat = route[:, :2].astype(I32).reshape(-1)
    onehot = (e_flat[:, None] == jnp.arange(N_EXPERTS, dtype=I32)[None, :]).astype(I32)
    csum = jnp.cumsum(onehot, axis=0)
    rank = jnp.sum(csum * onehot, axis=1) - 1
    counts = csum[-1]
    padded = ((counts + tb - 1) // tb) * tb
    pends = jnp.cumsum(padded)
    pstarts = pends - padded
    dest = jnp.sum(onehot * pstarts[None, :], axis=1) + rank
    max_blocks = (2 * n) // tb + N_EXPERTS
    n_used = (pends[-1] // tb).astype(I32).reshape(1)
    return dest.astype(I32), (pstarts // tb).astype(I32), (padded // tb).astype(I32), n_used, max_blocks * tb


def route_weights(w_group, b_group, w_expert, b_expert):
    d = w_group.shape[0]
    n_pad = LANES - N_GROUPS - N_EXPERTS
    w = jnp.concatenate([w_group, w_expert, jnp.zeros((d, n_pad), F32)], axis=1).astype(BF16)
    b = jnp.concatenate([b_group, b_expert, jnp.zeros((n_pad,), F32)]).reshape(1, LANES)
    return w, b


def hier_moe_ln(x, x_pk, route, w_gate, w_up, w_down, g, b, *, layer, tb=256, tm=256):
    dest, first_block, n_blocks, n_used, rows = routing_tables(route, tb=tb)
    xs = moe_dispatch(x_pk, dest, rows, tm=2 * tm)
    ys = moe_experts(xs, first_block, n_blocks, n_used, w_gate, w_up, w_down, layer=layer, tb=tb)
    return moe_combine_ln(x, route, dest, ys, g, b, tm=tm)


def pool_band_layer(x, w_in, w_pool, pool_scale, rel_bias, w_o, g, b, w_route, b_route, *, batch, seq):
    qscale = jnp.concatenate([jnp.ones((POOL_WIDTH,), F32), jnp.full((ATT_WIDTH,), ATT_HEAD_DIM ** -0.5 * LOG2_E, F32),
                              jnp.ones((2 * ATT_WIDTH,), F32)])
    h = matmul(x, (w_in * qscale).astype(BF16), tm=512, tn=1024, out_dtype=BF16)
    a_out = pool_mixer(h, w_pool.astype(BF16), pool_scale.reshape(1, -1), seq=seq, tm=512)
    b_out = band_attention(h, band_bias_table(rel_bias), batch=batch, seq=seq, tq=512, pairs=4)
    w_o = w_o.astype(BF16)
    return proj_residual_ln([a_out, b_out], [w_o[:POOL_WIDTH], w_o[POOL_WIDTH:]], x, g, b, w_route, b_route, tm=512)


def mla_layer(x, w_in, g_q, g_kv, w_uq, w_uk, w_uv, w_o, g, b, w_route, b_route, *, batch, seq):
    d = x.shape[1]
    tabs = rope_tables(seq)
    w_in_pad = jnp.concatenate([w_in, jnp.zeros((d, LANES - ROPE_DIM), F32)], axis=1).astype(BF16)
    cq, ckv, k_rope = mla_in_proj(x, w_in_pad, g_q, g_kv, tabs, seq=seq, tm=512)
    w_uq_h = w_uq.reshape(Q_LORA, MLA_HEADS, NOPE_DIM + ROPE_DIM)
    w_uq_cat = jnp.concatenate([w_uq_h, jnp.zeros((Q_LORA, MLA_HEADS, 2 * LANES - NOPE_DIM - ROPE_DIM), F32)], axis=2)
    w_uq_cat = w_uq_cat.reshape(Q_LORA, MLA_HEADS * 2 * LANES).astype(BF16)
    q_cat = mla_q_proj(cq, w_uq_cat, tabs, seq=seq, tm=512, heads_per_step=4)
    kv = matmul(ckv, jnp.concatenate([w_uk, w_uv], axis=1).astype(BF16), tm=512, tn=1024, out_dtype=BF16)
    o = mla_attention(q_cat, kv, k_rope, batch=batch, seq=seq, tq=512, tk=512, heads=8)
    return proj_residual_ln([o], [w_o.astype(BF16)], x, g, b, w_route, b_route, tm=512)


def kernel(x, ab_w_in, ab_w_pool, ab_pool_scale, ab_rel_bias, ab_w_o, mla_w_in, mla_g_q, mla_g_kv, mla_w_uq, mla_w_uk, mla_w_uv, mla_w_o, ln1_g, ln1_b, moe_w_group, moe_b_group, moe_w_expert, moe_b_expert, moe_w_gate, moe_w_up, moe_w_down, ln2_g, ln2_b):
    batch, seq, d = x.shape
    x = x.reshape(batch * seq, d)
    for i in range(DEPTH):
        j = i // 2
        w_route, b_route = route_weights(moe_w_group[i], moe_b_group[i], moe_w_expert[i], moe_b_expert[i])
        if i % 2 == 0:
            x, x_pk, route = pool_band_layer(x, ab_w_in[j], ab_w_pool[j], ab_pool_scale[j], ab_rel_bias[j], ab_w_o[j],
                                             ln1_g[i], ln1_b[i], w_route, b_route, batch=batch, seq=seq)
        else:
            x, x_pk, route = mla_layer(x, mla_w_in[j], mla_g_q[j], mla_g_kv[j], mla_w_uq[j], mla_w_uk[j], mla_w_uv[j],
                                       mla_w_o[j], ln1_g[i], ln1_b[i], w_route, b_route, batch=batch, seq=seq)
        x = hier_moe_ln(x, x_pk, route, moe_w_gate, moe_w_up, moe_w_down, ln2_g[i], ln2_b[i], layer=i)
    return x.reshape(batch, seq, d)
```

```python
import functools

import jax
import jax.numpy as jnp
from jax import lax
from jax.experimental import pallas as pl
from jax.experimental.pallas import tpu as pltpu

F32 = jnp.float32
BF16 = jnp.bfloat16
I32 = jnp.int32
U32 = jnp.uint32

D_MODEL = 2048
DEPTH = 4
CHUNK = 64
CHUNK_SHIFT = 6
GROUP_SHIFT = 3
POOL_WINDOWS = (2, 4, 8, 16)
POOL_GROUP_DIM = 256
POOL_WIDTH = 1024
ATT_WIDTH = 1024
ATT_HEADS = 16
ATT_HEAD_DIM = 64
LEFT_CHUNKS = 8
REL_MAX = 128
MLA_HEADS = 16
Q_LORA = 512
KV_LORA = 512
NOPE_DIM = 128
ROPE_DIM = 64
V_DIM = 128
ROPE_THETA = 10000.0
N_GROUPS = 8
EXPERTS_PER_GROUP = 8
N_EXPERTS = 64
EXPERT_HIDDEN = 512
DEEPNORM_ALPHA = (2 * DEPTH) ** 0.25
LN_EPS = 1e-5
RMS_EPS = 1e-6

LANES = 128
PACK_ROWS = D_MODEL // 2 // LANES
PROJ_SPLIT = 2
WEIGHT_RING = 4
POOL_HALO = 16
BAND_SUB = 128
BAND_KEYS = BAND_SUB + LEFT_CHUNKS * CHUNK
NEG_INF = float("-inf")
LOG2_E = 1.4426950408889634
VMEM_LIMIT = 56 * 1024 * 1024


def _cparams(sem, vmem=VMEM_LIMIT):
    return pltpu.CompilerParams(dimension_semantics=sem, vmem_limit_bytes=vmem)


def _mm_kernel(x_ref, w_ref, o_ref):
    o_ref[...] = jnp.dot(x_ref[...].astype(BF16), w_ref[...], preferred_element_type=F32).astype(o_ref.dtype)


def matmul(x, w, *, tm, tn, out_dtype):
    m, k = x.shape
    n = w.shape[1]
    return pl.pallas_call(
        _mm_kernel,
        grid=(m // tm, n // tn),
        in_specs=[pl.BlockSpec((tm, k), lambda i, j: (i, 0)), pl.BlockSpec((k, tn), lambda i, j: (0, j))],
        out_specs=pl.BlockSpec((tm, tn), lambda i, j: (i, j)),
        out_shape=jax.ShapeDtypeStruct((m, n), out_dtype),
        compiler_params=_cparams(("parallel", "parallel")),
        name="matmul",
    )(x, w)


def _pool_kernel(u_ref, halo_ref, w_ref, scale_ref, o_ref, *, tm, seq):
    i = pl.program_id(0)
    pos0 = (i * tm) % seq
    u = u_ref[...].astype(F32)
    halo = jnp.where(pos0 == 0, 0.0, halo_ref[...].astype(F32))
    ue = jnp.concatenate([halo, u], axis=0)
    pos = pos0 + lax.broadcasted_iota(I32, (tm, 1), 0)
    for g, win in enumerate(POOL_WINDOWS):
        cols = slice(g * POOL_GROUP_DIM, (g + 1) * POOL_GROUP_DIM)
        s = ue[:, cols]
        shift = 1
        while shift < win:
            s = s + pltpu.roll(s, shift, axis=0)
            shift *= 2
        wsum = s[POOL_HALO:, :]
        count = jnp.minimum(pos + 1, win).astype(F32)
        d = wsum / count - u[:, cols]
        y = jnp.dot(d.astype(BF16), w_ref[g], preferred_element_type=F32) * scale_ref[:, cols]
        o_ref[:, cols] = y.astype(o_ref.dtype)


def pool_mixer(h, w_pool, pool_scale, *, seq, tm):
    n = h.shape[0]
    hb = tm // POOL_HALO
    return pl.pallas_call(
        functools.partial(_pool_kernel, tm=tm, seq=seq),
        grid=(n // tm,),
        in_specs=[
            pl.BlockSpec((tm, POOL_WIDTH), lambda i: (i, 0)),
            pl.BlockSpec((POOL_HALO, POOL_WIDTH), lambda i: (jnp.maximum(i * hb - 1, 0), 0)),
            pl.BlockSpec((len(POOL_WINDOWS), POOL_GROUP_DIM, POOL_GROUP_DIM), lambda i: (0, 0, 0)),
            pl.BlockSpec((1, POOL_WIDTH), lambda i: (0, 0)),
        ],
        out_specs=pl.BlockSpec((tm, POOL_WIDTH), lambda i: (i, 0)),
        out_shape=jax.ShapeDtypeStruct((n, POOL_WIDTH), BF16),
        compiler_params=_cparams(("parallel",)),
        name="pool_mixer",
    )(h, h, w_pool, pool_scale)


def _band_kernel(q_ref, kp_ref, kc_ref, vp_ref, vc_ref, bias_ref, o_ref, *, tq, pairs):
    i = pl.program_id(1)
    lane = lax.broadcasted_iota(I32, (1, LANES), 1)
    first_head = lane < ATT_HEAD_DIM
    col = lax.broadcasted_iota(I32, (1, BAND_KEYS), 1)
    left = LEFT_CHUNKS * CHUNK
    ones = jnp.ones((BAND_KEYS, LANES), BF16)
    for pr in range(pairs):
        lanes = slice(pr * LANES, (pr + 1) * LANES)
        kk = jnp.concatenate([kp_ref[:, lanes], kc_ref[:, lanes]], axis=0)
        vv = jnp.concatenate([vp_ref[:, lanes], vc_ref[:, lanes]], axis=0)
        for j in range(tq // BAND_SUB):
            rows = slice(j * BAND_SUB, (j + 1) * BAND_SUB)
            qs = q_ref[rows, lanes]
            off = tq - left + j * BAND_SUB
            kw = kk[off:off + BAND_KEYS, :]
            vw = jnp.concatenate([vv[off:off + BAND_KEYS, :], ones], axis=1)
            valid = (i * tq + j * BAND_SUB - left + col) >= 0
            outs = []
            for hh in range(2):
                qm = jnp.where(first_head if hh == 0 else jnp.logical_not(first_head), qs, jnp.zeros_like(qs))
                s = lax.dot_general(qm, kw, (((1,), (1,)), ((), ())), preferred_element_type=F32)
                s = jnp.where(valid, s + bias_ref[2 * pr + hh], NEG_INF)
                m = jnp.max(s, axis=1, keepdims=True)
                p = jnp.exp2(s - m)
                o = jnp.dot(p.astype(BF16), vw, preferred_element_type=F32)
                outs.append(o[:, :LANES] / o[:, LANES:])
            o_ref[rows, lanes] = jnp.where(first_head, outs[0], outs[1]).astype(o_ref.dtype)


def band_bias_table(rel_bias):
    heads, rel_size = rel_bias.shape
    r = jnp.arange(BAND_SUB, dtype=I32)[:, None]
    c = jnp.arange(BAND_KEYS, dtype=I32)[None, :]
    qc, kc = r // CHUNK, c // CHUNK
    allowed = (kc >= qc) & (kc <= qc + LEFT_CHUNKS)
    n_lo = BAND_SUB - CHUNK
    n_hi = BAND_KEYS - 1 - REL_MAX
    by_dist = jnp.concatenate([jnp.repeat(rel_bias[:, :1], n_lo, axis=1), rel_bias,
                               jnp.repeat(rel_bias[:, -1:], n_hi, axis=1)], axis=1).astype(F32)
    period = BAND_SUB + BAND_KEYS - 1
    w = jnp.concatenate([by_dist[:, :BAND_KEYS][:, ::-1], by_dist[:, BAND_KEYS:][:, ::-1]], axis=1)
    rows = jnp.tile(w, (1, BAND_SUB))[:, :BAND_SUB * (period - 1)].reshape(heads, BAND_SUB, period - 1)
    return jnp.where(allowed[None], rows[:, :, :BAND_KEYS] * LOG2_E, NEG_INF)


def band_attention(h, bias_tab, *, batch, seq, tq, pairs):
    n = h.shape[0]
    nq = seq // tq
    width = pairs * LANES
    qcol = POOL_WIDTH // width
    kcol = qcol + ATT_WIDTH // width
    vcol = kcol + ATT_WIDTH // width
    cur = lambda c0: (lambda b, i, p: (b * nq + i, c0 + p))
    prev = lambda c0: (lambda b, i, p: (b * nq + jnp.maximum(i - 1, 0), c0 + p))
    blk = (tq, width)
    return pl.pallas_call(
        functools.partial(_band_kernel, tq=tq, pairs=pairs),
        grid=(batch, nq, ATT_HEADS // (2 * pairs)),
        in_specs=[
            pl.BlockSpec(blk, cur(qcol)),
            pl.BlockSpec(blk, prev(kcol)), pl.BlockSpec(blk, cur(kcol)),
            pl.BlockSpec(blk, prev(vcol)), pl.BlockSpec(blk, cur(vcol)),
            pl.BlockSpec((2 * pairs, BAND_SUB, BAND_KEYS), lambda b, i, p: (p, 0, 0)),
        ],
        out_specs=pl.BlockSpec(blk, lambda b, i, p: (b * nq + i, p)),
        out_shape=jax.ShapeDtypeStruct((n, ATT_WIDTH), BF16),
        compiler_params=_cparams(("parallel", "parallel", "parallel")),
        name="band_attention",
    )(h, h, h, h, h, bias_tab)


def _layer_norm_rows(z, g, b):
    mu = jnp.mean(z, axis=-1, keepdims=True)
    zc = z - mu
    var = jnp.mean(zc * zc, axis=-1, keepdims=True)
    return zc * lax.rsqrt(var + LN_EPS) * g + b


def _pack_tokens(z, ref):
    tm, d = z.shape
    lo = lax.bitcast_convert_type(z[:, :d // 2].astype(BF16).astype(F32), U32)
    hi = lax.bitcast_convert_type(z[:, d // 2:].astype(BF16).astype(F32), U32)
    pk = lax.shift_right_logical(lo, jnp.uint32(16)) | (hi & jnp.uint32(0xFFFF0000))
    for s in range(PACK_ROWS):
        ref[pl.ds(s, tm, stride=PACK_ROWS), :] = pk[:, s * LANES:(s + 1) * LANES]


def _unpack_tokens(ref, first_row, tm):
    pk = jnp.concatenate([ref[pl.ds(first_row + s, tm, stride=PACK_ROWS), :] for s in range(PACK_ROWS)], axis=1)
    lo = lax.bitcast_convert_type(lax.shift_left(pk, jnp.uint32(16)), F32)
    hi = lax.bitcast_convert_type(pk & jnp.uint32(0xFFFF0000), F32)
    return lo, hi


def _proj_ln_kernel(*refs, n_in):
    a_refs, w_refs = refs[:n_in], refs[n_in:2 * n_in]
    x_ref, g_ref, b_ref, wr_ref, br_ref, o_ref, opk_ref, route_ref = refs[2 * n_in:]
    tm = x_ref.shape[0]
    sub = tm // PROJ_SPLIT
    for r in range(PROJ_SPLIT):
        rows = pl.ds(r * sub, sub)
        y = jnp.dot(a_refs[0][rows, :], w_refs[0][...], preferred_element_type=F32)
        for a_ref, w_ref in zip(a_refs[1:], w_refs[1:]):
            y = y + jnp.dot(a_ref[rows, :], w_ref[...], preferred_element_type=F32)
        z = DEEPNORM_ALPHA * x_ref[rows, :] + y
        xn = _layer_norm_rows(z, g_ref[...], b_ref[...])
        o_ref[rows, :] = xn
        _pack_tokens(xn, opk_ref.at[pl.ds(r * sub * PACK_ROWS, sub * PACK_ROWS)])
        route_ref[rows, :] = _route_rows(xn, wr_ref, br_ref)


def proj_residual_ln(acts, weights, x, g, b, w_route, b_route, *, tm):
    n, d = x.shape
    n_in = len(acts)
    const = lambda i: (0, 0)
    in_specs = [pl.BlockSpec((tm, a.shape[1]), lambda i: (i, 0)) for a in acts]
    in_specs += [pl.BlockSpec(w.shape, const, pipeline_mode=pl.Buffered(1)) for w in weights]
    in_specs += [pl.BlockSpec((tm, d), lambda i: (i, 0)), pl.BlockSpec((1, d), const), pl.BlockSpec((1, d), const),
                 pl.BlockSpec((d, LANES), const), pl.BlockSpec((1, LANES), const)]
    return pl.pallas_call(
        functools.partial(_proj_ln_kernel, n_in=n_in),
        grid=(n // tm,),
        in_specs=in_specs,
        out_specs=[pl.BlockSpec((tm, d), lambda i: (i, 0)), pl.BlockSpec((tm * PACK_ROWS, LANES), lambda i: (i, 0)),
                   pl.BlockSpec((tm, LANES), lambda i: (i, 0))],
        out_shape=[jax.ShapeDtypeStruct((n, d), F32), jax.ShapeDtypeStruct((n * PACK_ROWS, LANES), U32),
                   jax.ShapeDtypeStruct((n, LANES), F32)],
        compiler_params=_cparams(("parallel",)),
        name="proj_residual_ln",
    )(*acts, *weights, x, g.reshape(1, d), b.reshape(1, d), w_route, b_route)


def _rope_lanes(x, c_ref, s_up_ref, s_dn_ref):
    half = ROPE_DIM // 2
    return (x * c_ref[...] + pltpu.roll(x, half, axis=1) * s_up_ref[...]
            + pltpu.roll(x, LANES - half, axis=1) * s_dn_ref[...])


def _rms_rows(x, g):
    return x * lax.rsqrt(jnp.mean(x * x, axis=-1, keepdims=True) + RMS_EPS) * g


def _mla_in_kernel(x_ref, w_ref, gq_ref, gkv_ref, c_ref, su_ref, sd_ref, cq_ref, ckv_ref, kr_ref):
    h = jnp.dot(x_ref[...].astype(BF16), w_ref[...], preferred_element_type=F32)
    cq_ref[...] = _rms_rows(h[:, :Q_LORA], gq_ref[...]).astype(cq_ref.dtype)
    ckv_ref[...] = _rms_rows(h[:, Q_LORA:Q_LORA + KV_LORA], gkv_ref[...]).astype(ckv_ref.dtype)
    kr_ref[...] = _rope_lanes(h[:, Q_LORA + KV_LORA:], c_ref, su_ref, sd_ref).astype(kr_ref.dtype)


def mla_in_proj(x, w_in_pad, g_q, g_kv, rope_tabs, *, seq, tm):
    n, d = x.shape
    wn = w_in_pad.shape[1]
    npos = seq // tm
    tab = pl.BlockSpec((tm, LANES), lambda i: (i % npos, 0))
    return pl.pallas_call(
        _mla_in_kernel,
        grid=(n // tm,),
        in_specs=[pl.BlockSpec((tm, d), lambda i: (i, 0)), pl.BlockSpec((d, wn), lambda i: (0, 0)),
                  pl.BlockSpec((1, Q_LORA), lambda i: (0, 0)), pl.BlockSpec((1, KV_LORA), lambda i: (0, 0)),
                  tab, tab, tab],
        out_specs=[pl.BlockSpec((tm, Q_LORA), lambda i: (i, 0)), pl.BlockSpec((tm, KV_LORA), lambda i: (i, 0)),
                   pl.BlockSpec((tm, LANES), lambda i: (i, 0))],
        out_shape=[jax.ShapeDtypeStruct((n, Q_LORA), BF16), jax.ShapeDtypeStruct((n, KV_LORA), BF16),
                   jax.ShapeDtypeStruct((n, LANES), BF16)],
        compiler_params=_cparams(("parallel",)),
        name="mla_in_proj",
    )(x, w_in_pad, g_q.reshape(1, -1), g_kv.reshape(1, -1), *rope_tabs)


def _mla_q_kernel(cq_ref, w_ref, c_ref, su_ref, sd_ref, o_ref, *, heads):
    cq = cq_ref[...]
    for hh in range(heads):
        base = hh * 2 * LANES
        q = jnp.dot(cq, w_ref[:, base:base + 2 * LANES], preferred_element_type=F32)
        o_ref[:, base:base + LANES] = q[:, :LANES].astype(o_ref.dtype)
        o_ref[:, base + LANES:base + 2 * LANES] = _rope_lanes(q[:, LANES:], c_ref, su_ref, sd_ref).astype(o_ref.dtype)


def mla_q_proj(cq, w_uq_cat, rope_tabs, *, seq, tm, heads_per_step):
    n = cq.shape[0]
    wn = w_uq_cat.shape[1]
    tn = heads_per_step * 2 * LANES
    npos = seq // tm
    tab = pl.BlockSpec((tm, LANES), lambda i, j: (i % npos, 0))
    return pl.pallas_call(
        functools.partial(_mla_q_kernel, heads=heads_per_step),
        grid=(n // tm, wn // tn),
        in_specs=[pl.BlockSpec((tm, Q_LORA), lambda i, j: (i, 0)), pl.BlockSpec((Q_LORA, tn), lambda i, j: (0, j)),
                  tab, tab, tab],
        out_specs=pl.BlockSpec((tm, tn), lambda i, j: (i, j)),
        out_shape=jax.ShapeDtypeStruct((n, wn), BF16),
        compiler_params=_cparams(("parallel", "parallel")),
        name="mla_q_proj",
    )(cq, w_uq_cat, *rope_tabs)


def _mla_attn_kernel(q_ref, kn_ref, kr_ref, v_ref, o_ref, m_sc, acc_sc, *, tq, tk, heads):
    qi = pl.program_id(2)
    n_full = qi * (tq // tk)
    m_sc[...] = jnp.full_like(m_sc, NEG_INF)
    acc_sc[...] = jnp.zeros_like(acc_sc)
    ones = jnp.ones((tk, LANES), BF16)
    row_chunk = jnp.right_shift(lax.broadcasted_iota(I32, (tq, 1), 0), CHUNK_SHIFT)
    col_chunk = jnp.right_shift(lax.broadcasted_iota(I32, (1, tk), 1), CHUNK_SHIFT)

    def step(kj, diag):
        off = pl.multiple_of(kj * tk, tk)
        kr = kr_ref[pl.ds(off, tk), :]
        for hh in range(heads):
            k = jnp.concatenate([kn_ref[pl.ds(off, tk), hh * NOPE_DIM:(hh + 1) * NOPE_DIM], kr], axis=1)
            s = lax.dot_general(q_ref[:, hh * 2 * LANES:(hh + 1) * 2 * LANES], k, (((1,), (1,)), ((), ())),
                                preferred_element_type=F32)
            if diag is not None:
                s = jnp.where(col_chunk + diag * (tk // CHUNK) <= row_chunk, s, NEG_INF)
            m_prev = m_sc[hh]
            m_new = jnp.maximum(m_prev, jnp.max(s, axis=1, keepdims=True))
            alpha = jnp.exp2(m_prev - m_new)
            p = jnp.exp2(s - jnp.tile(m_new, (1, tk // LANES)))
            va = jnp.concatenate([v_ref[pl.ds(off, tk), hh * V_DIM:(hh + 1) * V_DIM], ones], axis=1)
            pv = jnp.dot(p.astype(BF16), va, preferred_element_type=F32)
            acc_sc[hh] = jnp.tile(alpha, (1, 2)) * acc_sc[hh] + pv
            m_sc[hh] = m_new

    def two_full_steps(i, c):
        step(2 * i, None)
        step(2 * i + 1, None)
        return c

    lax.fori_loop(0, n_full // 2, two_full_steps, 0)

    @pl.when(n_full % 2 == 1)
    def _():
        step(n_full - 1, None)

    for dj in range(tq // tk):
        step(n_full + dj, dj)
    for hh in range(heads):
        acc = acc_sc[hh]
        o_ref[:, hh * V_DIM:(hh + 1) * V_DIM] = (acc[:, :V_DIM] / acc[:, V_DIM:]).astype(o_ref.dtype)


def mla_attention(q_cat, kv, k_rope, *, batch, seq, tq, tk, heads):
    n = q_cat.shape[0]
    nq = seq // tq
    n_hb = MLA_HEADS // heads
    return pl.pallas_call(
        functools.partial(_mla_attn_kernel, tq=tq, tk=tk, heads=heads),
        grid=(batch, n_hb, nq),
        in_specs=[
            pl.BlockSpec((tq, heads * 2 * LANES), lambda b, h, qi: (b * nq + qi, h)),
            pl.BlockSpec((seq, heads * NOPE_DIM), lambda b, h, qi: (b, h), pipeline_mode=pl.Buffered(1)),
            pl.BlockSpec((seq, LANES), lambda b, h, qi: (b, 0), pipeline_mode=pl.Buffered(1)),
            pl.BlockSpec((seq, heads * V_DIM), lambda b, h, qi: (b, n_hb + h), pipeline_mode=pl.Buffered(1)),
        ],
        out_specs=pl.BlockSpec((tq, heads * V_DIM), lambda b, h, qi: (b * nq + qi, h)),
        out_shape=jax.ShapeDtypeStruct((n, MLA_HEADS * V_DIM), BF16),
        scratch_shapes=[pltpu.VMEM((heads, tq, LANES), F32), pltpu.VMEM((heads, tq, 2 * LANES), F32)],
        compiler_params=_cparams(("parallel", "parallel", "arbitrary")),
        name="mla_attention",
    )(q_cat, kv, k_rope, kv)


def rope_tables(seq):
    half = ROPE_DIM // 2
    freq = ROPE_THETA ** (-jnp.arange(half, dtype=F32) / half)
    ang = jnp.arange(seq, dtype=F32)[:, None] * freq[None, :]
    cos, sin = jnp.cos(ang), jnp.sin(ang)
    z = jnp.zeros_like(cos)
    c = jnp.concatenate([cos, cos, z, z], axis=1)
    s_up = jnp.concatenate([z, sin, z, z], axis=1)
    s_dn = jnp.concatenate([-sin, z, z, z], axis=1)
    return c, s_up, s_dn


def _route_rows(xn, w_ref, b_ref):
    logits = jnp.dot(xn.astype(BF16), w_ref[...], preferred_element_type=F32) + b_ref[...]
    tm = logits.shape[0]
    lane = lax.broadcasted_iota(I32, (tm, LANES), 1)
    is_group = lane < N_GROUPS
    lg = jnp.where(is_group, logits, NEG_INF)
    mg = jnp.max(lg, axis=1, keepdims=True)
    g_sel = jnp.min(jnp.where(lg == mg, lane, LANES), axis=1, keepdims=True)
    p_group = 1.0 / jnp.sum(jnp.where(is_group, jnp.exp(lg - mg), 0.0), axis=1, keepdims=True)
    e_lane = lane - N_GROUPS
    in_sel = (e_lane >= 0) & (e_lane < N_EXPERTS) & (jnp.right_shift(e_lane, GROUP_SHIFT) == g_sel)
    le = jnp.where(in_sel, logits, NEG_INF)
    v1 = jnp.max(le, axis=1, keepdims=True)
    i1 = jnp.min(jnp.where(le == v1, lane, LANES), axis=1, keepdims=True)
    le2 = jnp.where(lane == i1, NEG_INF, le)
    v2 = jnp.max(le2, axis=1, keepdims=True)
    i2 = jnp.min(jnp.where(le2 == v2, lane, LANES), axis=1, keepdims=True)
    t = jnp.exp(v2 - v1)
    p1 = 1.0 / (1.0 + t)
    p2 = t / (1.0 + t)
    out = jnp.where(lane == 0, (i1 - N_GROUPS).astype(F32), 0.0)
    out = jnp.where(lane == 1, (i2 - N_GROUPS).astype(F32), out)
    out = jnp.where(lane == 2, p_group * p1, out)
    out = jnp.where(lane == 3, p_group * p2, out)
    return out


def _token_copy(src, src_tok, dst, dst_tok, sem):
    def first_row(tok):
        row = tok * PACK_ROWS
        return row if isinstance(row, int) else pl.multiple_of(row, PACK_ROWS)

    return pltpu.make_async_copy(src.at[pl.ds(first_row(src_tok), PACK_ROWS)],
                                 dst.at[pl.ds(first_row(dst_tok), PACK_ROWS)], sem)


def _dispatch_kernel(dest_ref, x_ref, init_ref, xs_ref, sem, *, tm):
    del init_ref
    base = pl.program_id(0) * tm * 2

    copies = [_token_copy(x_ref, a // 2, xs_ref, dest_ref[base + a], sem) for a in range(2 * tm)]
    for a, cp in enumerate(copies):
        cp.start(priority=a % 2)
    for cp in copies:
        cp.wait()


def moe_dispatch(x_pk, dest, rows, *, tm):
    n = x_pk.shape[0] // PACK_ROWS
    return pl.pallas_call(
        functools.partial(_dispatch_kernel, tm=tm),
        grid_spec=pltpu.PrefetchScalarGridSpec(
            num_scalar_prefetch=1,
            grid=(n // tm,),
            in_specs=[pl.BlockSpec((tm * PACK_ROWS, LANES), lambda i, dest: (i, 0)), pl.BlockSpec(memory_space=pl.ANY)],
            out_specs=pl.BlockSpec(memory_space=pl.ANY),
            scratch_shapes=[pltpu.SemaphoreType.DMA(())],
        ),
        out_shape=jax.ShapeDtypeStruct((rows * PACK_ROWS, LANES), U32),
        input_output_aliases={2: 0},
        compiler_params=_cparams(("arbitrary",)),
        name="moe_dispatch",
    )(dest, x_pk, jnp.zeros((rows * PACK_ROWS, LANES), U32))


def _expert_kernel(b0_ref, nb_ref, nu_ref, xs_ref, wg_hbm, wu_hbm, wd_hbm, ys_ref,
                   xbuf, ybuf, wg_buf, wu_buf, wd_buf, sem_in, sem_out, sem_w, *, tb, layer):
    e = pl.program_id(0)
    n_experts = pl.num_programs(0)
    b0, nb, n_used = b0_ref[e], nb_ref[e], nu_ref[0]
    rows = tb * PACK_ROWS

    def weight_copies(ex):
        slot = lax.rem(ex, WEIGHT_RING)
        return [pltpu.make_async_copy(hbm.at[layer, ex], buf.at[slot], sem_w.at[k, slot])
                for k, (hbm, buf) in enumerate(((wg_hbm, wg_buf), (wu_hbm, wu_buf), (wd_hbm, wd_buf)))]

    def in_copy(g):
        src = xs_ref.at[pl.ds(pl.multiple_of(g * rows, rows), rows)]
        return pltpu.make_async_copy(src, xbuf.at[g % 2], sem_in.at[g % 2])

    def out_copy(g):
        dst = ys_ref.at[pl.ds(pl.multiple_of(g * rows, rows), rows)]
        return pltpu.make_async_copy(ybuf.at[g % 2], dst, sem_out.at[g % 2])

    @pl.when(e == 0)
    def _():
        in_copy(0).start()
        for ex in range(WEIGHT_RING - 1):
            for cp in weight_copies(ex):
                cp.start()

    @pl.when(e + WEIGHT_RING - 1 < n_experts)
    def _():
        for cp in weight_copies(e + WEIGHT_RING - 1):
            cp.start()

    for cp in weight_copies(e):
        cp.wait()
    slot = lax.rem(e, WEIGHT_RING)

    @pl.when(nb > 0)
    def _():
        def block(g, c):
            @pl.when(g + 1 < n_used)
            def _():
                in_copy(g + 1).start()

            in_copy(g).wait()

            @pl.when(g >= 2)
            def _():
                out_copy(g - 2).wait()

            lo, hi = _unpack_tokens(xbuf.at[g % 2], 0, tb)
            xb = jnp.concatenate([lo.astype(BF16), hi.astype(BF16)], axis=1)
            hg = jnp.dot(xb, wg_buf[slot].astype(BF16), preferred_element_type=F32)
            hu = jnp.dot(xb, wu_buf[slot].astype(BF16), preferred_element_type=F32)
            hb = (hg * jax.nn.sigmoid(hg) * hu).astype(BF16)
            _pack_tokens(jnp.dot(hb, wd_buf[slot].astype(BF16), preferred_element_type=F32), ybuf.at[g % 2])
            out_copy(g).start()
            return c

        lax.fori_loop(b0, b0 + nb, block, 0)

    @pl.when(e == n_experts - 1)
    def _():
        @pl.when(n_used >= 2)
        def _():
            out_copy(n_used - 2).wait()

        out_copy(n_used - 1).wait()


def moe_experts(xs, first_block, n_blocks, n_used, w_gate, w_up, w_down, *, layer, tb):
    _, n_experts, d, hdim = w_gate.shape
    assert n_experts >= WEIGHT_RING
    ring = pltpu.VMEM((2, tb * PACK_ROWS, LANES), U32)
    any_space = pl.BlockSpec(memory_space=pl.ANY)
    return pl.pallas_call(
        functools.partial(_expert_kernel, tb=tb, layer=layer),
        grid_spec=pltpu.PrefetchScalarGridSpec(
            num_scalar_prefetch=3,
            grid=(n_experts,),
            in_specs=[any_space, any_space, any_space, any_space],
            out_specs=any_space,
            scratch_shapes=[ring, ring, pltpu.VMEM((WEIGHT_RING, d, hdim), F32), pltpu.VMEM((WEIGHT_RING, d, hdim), F32),
                            pltpu.VMEM((WEIGHT_RING, hdim, d), F32), pltpu.SemaphoreType.DMA((2,)),
                            pltpu.SemaphoreType.DMA((2,)), pltpu.SemaphoreType.DMA((3, WEIGHT_RING))],
        ),
        out_shape=jax.ShapeDtypeStruct(xs.shape, xs.dtype),
        input_output_aliases={3: 0},
        compiler_params=_cparams(("arbitrary",)),
        name="moe_experts",
    )(first_block, n_blocks, n_used, xs, w_gate, w_up, w_down)


def _combine_ln_kernel(dest_ref, x_ref, route_ref, g_ref, b_ref, ys_ref, o_ref, buf, sem, *, tm):
    i = pl.program_id(0)

    def gathers(step):
        slot = step % 2
        return [_token_copy(ys_ref, dest_ref[step * 2 * tm + a], buf.at[slot], (a % 2) * tm + a // 2, sem.at[slot])
                for a in range(2 * tm)]

    def start(step):
        for a, cp in enumerate(gathers(step)):
            cp.start(priority=a % 2)

    @pl.when(i == 0)
    def _():
        start(0)

    @pl.when(i + 1 < pl.num_programs(0))
    def _():
        start(i + 1)

    for cp in gathers(i):
        cp.wait()
    cur = buf.at[i % 2]
    g1 = route_ref[:, 2:3]
    g2 = route_ref[:, 3:4]
    lo1, hi1 = _unpack_tokens(cur, 0, tm)
    lo2, hi2 = _unpack_tokens(cur, tm * PACK_ROWS, tm)
    f = jnp.concatenate([lo1 * g1 + lo2 * g2, hi1 * g1 + hi2 * g2], axis=1)
    z = DEEPNORM_ALPHA * x_ref[...] + f
    o_ref[...] = _layer_norm_rows(z, g_ref[...], b_ref[...])


def moe_combine_ln(x, route, dest, ys, g, b, *, tm):
    n, d = x.shape
    return pl.pallas_call(
        functools.partial(_combine_ln_kernel, tm=tm),
        grid_spec=pltpu.PrefetchScalarGridSpec(
            num_scalar_prefetch=1,
            grid=(n // tm,),
            in_specs=[pl.BlockSpec((tm, d), lambda i, dest: (i, 0)), pl.BlockSpec((tm, LANES), lambda i, dest: (i, 0)),
                      pl.BlockSpec((1, d), lambda i, dest: (0, 0)), pl.BlockSpec((1, d), lambda i, dest: (0, 0)),
                      pl.BlockSpec(memory_space=pl.ANY)],
            out_specs=pl.BlockSpec((tm, d), lambda i, dest: (i, 0)),
            scratch_shapes=[pltpu.VMEM((2, 2 * tm * PACK_ROWS, LANES), U32), pltpu.SemaphoreType.DMA((2,))],
        ),
        out_shape=jax.ShapeDtypeStruct((n, d), F32),
        compiler_params=_cparams(("arbitrary",)),
        name="moe_combine_ln",
    )(dest, x, route, g.reshape(1, d), b.reshape(1, d), ys)


def routing_tables(route, *, tb):
    n = route.shape[0]
    e_flat = route[:, :2].astype(I32).reshape(-1)
    onehot = (e_flat[:, None] == jnp.arange(N_EXPERTS, dtype=I32)[None, :]).astype(I32)
    csum = jnp.cumsum(onehot, axis=0)
    rank = jnp.sum(csum * onehot, axis=1) - 1
    counts = csum[-1]
    padded = ((counts + tb - 1) // tb) * tb
    pends = jnp.cumsum(padded)
    pstarts = pends - padded
    dest = jnp.sum(onehot * pstarts[None, :], axis=1) + rank
    max_blocks = (2 * n) // tb + N_EXPERTS
    n_used = (pends[-1] // tb).astype(I32).reshape(1)
    return dest.astype(I32), (pstarts // tb).astype(I32), (padded // tb).astype(I32), n_used, max_blocks * tb


def route_weights(w_group, b_group, w_expert, b_expert):
    d = w_group.shape[0]
    n_pad = LANES - N_GROUPS - N_EXPERTS
    w = jnp.concatenate([w_group, w_expert, jnp.zeros((d, n_pad), F32)], axis=1).astype(BF16)
    b = jnp.concatenate([b_group, b_expert, jnp.zeros((n_pad,), F32)]).reshape(1, LANES)
    return w, b


def hier_moe_ln(x, x_pk, route, w_gate, w_up, w_down, g, b, *, layer, tb=256, tm=256):
    dest, first_block, n_blocks, n_used, rows = routing_tables(route, tb=tb)
    xs = moe_dispatch(x_pk, dest, rows, tm=2 * tm)
    ys = moe_experts(xs, first_block, n_blocks, n_used, w_gate, w_up, w_down, layer=layer, tb=tb)
    return moe_combine_ln(x, route, dest, ys, g, b, tm=tm)


def pool_band_layer(x, w_in, w_pool, pool_scale, rel_bias, w_o, g, b, w_route, b_route, *, batch, seq):
    qscale = jnp.concatenate([jnp.ones((POOL_WIDTH,), F32), jnp.full((ATT_WIDTH,), ATT_HEAD_DIM ** -0.5 * LOG2_E, F32),
                              jnp.ones((2 * ATT_WIDTH,), F32)])
    h = matmul(x, (w_in * qscale).astype(BF16), tm=512, tn=1024, out_dtype=BF16)
    a_out = pool_mixer(h, w_pool.astype(BF16), pool_scale.reshape(1, -1), seq=seq, tm=512)
    b_out = band_attention(h, band_bias_table(rel_bias), batch=batch, seq=seq, tq=512, pairs=4)
    w_o = w_o.astype(BF16)
    return proj_residual_ln([a_out, b_out], [w_o[:POOL_WIDTH], w_o[POOL_WIDTH:]], x, g, b, w_route, b_route, tm=512)


def mla_layer(x, w_in, g_q, g_kv, w_uq, w_uk, w_uv, w_o, g, b, w_route, b_route, *, batch, seq):
    d = x.shape[1]
    tabs = rope_tables(seq)
    w_in_pad = jnp.concatenate([w_in, jnp.zeros((d, LANES - ROPE_DIM), F32)], axis=1).astype(BF16)
    cq, ckv, k_rope = mla_in_proj(x, w_in_pad, g_q, g_kv, tabs, seq=seq, tm=512)
    w_uq_h = w_uq.reshape(Q_LORA, MLA_HEADS, NOPE_DIM + ROPE_DIM)
    w_uq_cat = jnp.concatenate([w_uq_h, jnp.zeros((Q_LORA, MLA_HEADS, 2 * LANES - NOPE_DIM - ROPE_DIM), F32)], axis=2)
    q_scale = (NOPE_DIM + ROPE_DIM) ** -0.5 * LOG2_E
    w_uq_cat = (w_uq_cat.reshape(Q_LORA, MLA_HEADS * 2 * LANES) * q_scale).astype(BF16)
    q_cat = mla_q_proj(cq, w_uq_cat, tabs, seq=seq, tm=512, heads_per_step=MLA_HEADS)
    w_kv = jnp.concatenate([w_uk, w_uv], axis=1).astype(BF16)
    kv = matmul(ckv, w_kv, tm=512, tn=w_kv.shape[1], out_dtype=BF16)
    o = mla_attention(q_cat, kv, k_rope, batch=batch, seq=seq, tq=512, tk=512, heads=8)
    return proj_residual_ln([o], [w_o.astype(BF16)], x, g, b, w_route, b_route, tm=512)


def kernel(x, ab_w_in, ab_w_pool, ab_pool_scale, ab_rel_bias, ab_w_o, mla_w_in, mla_g_q, mla_g_kv, mla_w_uq, mla_w_uk, mla_w_uv, mla_w_o, ln1_g, ln1_b, moe_w_group, moe_b_group, moe_w_expert, moe_b_expert, moe_w_gate, moe_w_up, moe_w_down, ln2_g, ln2_b):
    batch, seq, d = x.shape
    x = x.reshape(batch * seq, d)
    for i in range(DEPTH):
        j = i // 2
        w_route, b_route = route_weights(moe_w_group[i], moe_b_group[i], moe_w_expert[i], moe_b_expert[i])
        if i % 2 == 0:
            x, x_pk, route = pool_band_layer(x, ab_w_in[j], ab_w_pool[j], ab_pool_scale[j], ab_rel_bias[j], ab_w_o[j],
                                             ln1_g[i], ln1_b[i], w_route, b_route, batch=batch, seq=seq)
        else:
            x, x_pk, route = mla_layer(x, mla_w_in[j], mla_g_q[j], mla_g_kv[j], mla_w_uq[j], mla_w_uk[j], mla_w_uv[j],
                                       mla_w_o[j], ln1_g[i], ln1_b[i], w_route, b_route, batch=batch, seq=seq)
        x = hier_moe_ln(x, x_pk, route, moe_w_gate, moe_w_up, moe_w_down, ln2_g[i], ln2_b[i], layer=i)
    return x.reshape(batch, seq, d)
```

```python
import functools

import jax
import jax.numpy as jnp
from jax import lax
from jax.experimental import pallas as pl
from jax.experimental.pallas import tpu as pltpu

F32 = jnp.float32
BF16 = jnp.bfloat16
I32 = jnp.int32
U32 = jnp.uint32

D_MODEL = 2048
DEPTH = 4
CHUNK = 64
CHUNK_SHIFT = 6
GROUP_SHIFT = 3
POOL_WINDOWS = (2, 4, 8, 16)
POOL_GROUP_DIM = 256
POOL_WIDTH = 1024
ATT_WIDTH = 1024
ATT_HEADS = 16
ATT_HEAD_DIM = 64
LEFT_CHUNKS = 8
REL_MAX = 128
MLA_HEADS = 16
Q_LORA = 512
KV_LORA = 512
NOPE_DIM = 128
ROPE_DIM = 64
V_DIM = 128
ROPE_THETA = 10000.0
N_GROUPS = 8
EXPERTS_PER_GROUP = 8
N_EXPERTS = 64
EXPERT_HIDDEN = 512
DEEPNORM_ALPHA = (2 * DEPTH) ** 0.25
LN_EPS = 1e-5
RMS_EPS = 1e-6

LANES = 128
PACK_ROWS = D_MODEL // 2 // LANES
PROJ_SPLIT = 2
WEIGHT_RING = 4
POOL_HALO = 16
BAND_SUB = 128
BAND_KEYS = BAND_SUB + LEFT_CHUNKS * CHUNK
NEG_INF = float("-inf")
LOG2_E = 1.4426950408889634
VMEM_LIMIT = 56 * 1024 * 1024


def _cparams(sem, vmem=VMEM_LIMIT):
    return pltpu.CompilerParams(dimension_semantics=sem, vmem_limit_bytes=vmem)


def _mm_kernel(x_ref, w_ref, o_ref):
    o_ref[...] = jnp.dot(x_ref[...].astype(BF16), w_ref[...], preferred_element_type=F32).astype(o_ref.dtype)


def matmul(x, w, *, tm, tn, out_dtype):
    m, k = x.shape
    n = w.shape[1]
    return pl.pallas_call(
        _mm_kernel,
        grid=(m // tm, n // tn),
        in_specs=[pl.BlockSpec((tm, k), lambda i, j: (i, 0)), pl.BlockSpec((k, tn), lambda i, j: (0, j))],
        out_specs=pl.BlockSpec((tm, tn), lambda i, j: (i, j)),
        out_shape=jax.ShapeDtypeStruct((m, n), out_dtype),
        compiler_params=_cparams(("parallel", "parallel")),
        name="matmul",
    )(x, w)


def _pool_kernel(u_ref, halo_ref, w_ref, scale_ref, o_ref, *, tm, seq):
    i = pl.program_id(0)
    pos0 = (i * tm) % seq
    u = u_ref[...].astype(F32)
    halo = jnp.where(pos0 == 0, 0.0, halo_ref[...].astype(F32))
    ue = jnp.concatenate([halo, u], axis=0)
    pos = pos0 + lax.broadcasted_iota(I32, (tm, 1), 0)
    for g, win in enumerate(POOL_WINDOWS):
        cols = slice(g * POOL_GROUP_DIM, (g + 1) * POOL_GROUP_DIM)
        s = ue[:, cols]
        shift = 1
        while shift < win:
            s = s + pltpu.roll(s, shift, axis=0)
            shift *= 2
        wsum = s[POOL_HALO:, :]
        count = jnp.minimum(pos + 1, win).astype(F32)
        d = wsum / count - u[:, cols]
        y = jnp.dot(d.astype(BF16), w_ref[g], preferred_element_type=F32) * scale_ref[:, cols]
        o_ref[:, cols] = y.astype(o_ref.dtype)


def pool_mixer(h, w_pool, pool_scale, *, seq, tm):
    n = h.shape[0]
    hb = tm // POOL_HALO
    return pl.pallas_call(
        functools.partial(_pool_kernel, tm=tm, seq=seq),
        grid=(n // tm,),
        in_specs=[
            pl.BlockSpec((tm, POOL_WIDTH), lambda i: (i, 0)),
            pl.BlockSpec((POOL_HALO, POOL_WIDTH), lambda i: (jnp.maximum(i * hb - 1, 0), 0)),
            pl.BlockSpec((len(POOL_WINDOWS), POOL_GROUP_DIM, POOL_GROUP_DIM), lambda i: (0, 0, 0)),
            pl.BlockSpec((1, POOL_WIDTH), lambda i: (0, 0)),
        ],
        out_specs=pl.BlockSpec((tm, POOL_WIDTH), lambda i: (i, 0)),
        out_shape=jax.ShapeDtypeStruct((n, POOL_WIDTH), BF16),
        compiler_params=_cparams(("parallel",)),
        name="pool_mixer",
    )(h, h, w_pool, pool_scale)


def _band_kernel(q_ref, kp_ref, kc_ref, vp_ref, vc_ref, bias_ref, o_ref, *, tq, pairs):
    i = pl.program_id(1)
    lane = lax.broadcasted_iota(I32, (1, LANES), 1)
    first_head = lane < ATT_HEAD_DIM
    col = lax.broadcasted_iota(I32, (1, BAND_KEYS), 1)
    left = LEFT_CHUNKS * CHUNK
    ones = jnp.ones((BAND_KEYS, LANES), BF16)
    for pr in range(pairs):
        lanes = slice(pr * LANES, (pr + 1) * LANES)
        kk = jnp.concatenate([kp_ref[:, lanes], kc_ref[:, lanes]], axis=0)
        vv = jnp.concatenate([vp_ref[:, lanes], vc_ref[:, lanes]], axis=0)
        for j in range(tq // BAND_SUB):
            rows = slice(j * BAND_SUB, (j + 1) * BAND_SUB)
            qs = q_ref[rows, lanes]
            off = tq - left + j * BAND_SUB
            kw = kk[off:off + BAND_KEYS, :]
            vw = jnp.concatenate([vv[off:off + BAND_KEYS, :], ones], axis=1)
            valid = (i * tq + j * BAND_SUB - left + col) >= 0
            outs = []
            for hh in range(2):
                qm = jnp.where(first_head if hh == 0 else jnp.logical_not(first_head), qs, jnp.zeros_like(qs))
                s = lax.dot_general(qm, kw, (((1,), (1,)), ((), ())), preferred_element_type=F32)
                s = jnp.where(valid, s + bias_ref[2 * pr + hh], NEG_INF)
                m = jnp.max(s, axis=1, keepdims=True)
                p = jnp.exp2(s - m)
                o = jnp.dot(p.astype(BF16), vw, preferred_element_type=F32)
                outs.append(o[:, :LANES] / o[:, LANES:])
            o_ref[rows, lanes] = jnp.where(first_head, outs[0], outs[1]).astype(o_ref.dtype)


def band_bias_table(rel_bias):
    heads, rel_size = rel_bias.shape
    r = jnp.arange(BAND_SUB, dtype=I32)[:, None]
    c = jnp.arange(BAND_KEYS, dtype=I32)[None, :]
    qc, kc = r // CHUNK, c // CHUNK
    allowed = (kc >= qc) & (kc <= qc + LEFT_CHUNKS)
    n_lo = BAND_SUB - CHUNK
    n_hi = BAND_KEYS - 1 - REL_MAX
    by_dist = jnp.concatenate([jnp.repeat(rel_bias[:, :1], n_lo, axis=1), rel_bias,
                               jnp.repeat(rel_bias[:, -1:], n_hi, axis=1)], axis=1).astype(F32)
    period = BAND_SUB + BAND_KEYS - 1
    w = jnp.concatenate([by_dist[:, :BAND_KEYS][:, ::-1], by_dist[:, BAND_KEYS:][:, ::-1]], axis=1)
    rows = jnp.tile(w, (1, BAND_SUB))[:, :BAND_SUB * (period - 1)].reshape(heads, BAND_SUB, period - 1)
    return jnp.where(allowed[None], rows[:, :, :BAND_KEYS] * LOG2_E, NEG_INF)


def band_attention(h, bias_tab, *, batch, seq, tq, pairs):
    n = h.shape[0]
    nq = seq // tq
    width = pairs * LANES
    qcol = POOL_WIDTH // width
    kcol = qcol + ATT_WIDTH // width
    vcol = kcol + ATT_WIDTH // width
    cur = lambda c0: (lambda b, i, p: (b * nq + i, c0 + p))
    prev = lambda c0: (lambda b, i, p: (b * nq + jnp.maximum(i - 1, 0), c0 + p))
    blk = (tq, width)
    return pl.pallas_call(
        functools.partial(_band_kernel, tq=tq, pairs=pairs),
        grid=(batch, nq, ATT_HEADS // (2 * pairs)),
        in_specs=[
            pl.BlockSpec(blk, cur(qcol)),
            pl.BlockSpec(blk, prev(kcol)), pl.BlockSpec(blk, cur(kcol)),
            pl.BlockSpec(blk, prev(vcol)), pl.BlockSpec(blk, cur(vcol)),
            pl.BlockSpec((2 * pairs, BAND_SUB, BAND_KEYS), lambda b, i, p: (p, 0, 0)),
        ],
        out_specs=pl.BlockSpec(blk, lambda b, i, p: (b * nq + i, p)),
        out_shape=jax.ShapeDtypeStruct((n, ATT_WIDTH), BF16),
        compiler_params=_cparams(("parallel", "parallel", "parallel")),
        name="band_attention",
    )(h, h, h, h, h, bias_tab)


def _layer_norm_rows(z, g, b):
    mu = jnp.mean(z, axis=-1, keepdims=True)
    zc = z - mu
    var = jnp.mean(zc * zc, axis=-1, keepdims=True)
    return zc * lax.rsqrt(var + LN_EPS) * g + b


def _pack_tokens(z, ref):
    tm, d = z.shape
    lo = lax.bitcast_convert_type(z[:, :d // 2].astype(BF16).astype(F32), U32)
    hi = lax.bitcast_convert_type(z[:, d // 2:].astype(BF16).astype(F32), U32)
    pk = lax.shift_right_logical(lo, jnp.uint32(16)) | (hi & jnp.uint32(0xFFFF0000))
    for s in range(PACK_ROWS):
        ref[pl.ds(s, tm, stride=PACK_ROWS), :] = pk[:, s * LANES:(s + 1) * LANES]


def _unpack_tokens(ref, first_row, tm):
    pk = jnp.concatenate([ref[pl.ds(first_row + s, tm, stride=PACK_ROWS), :] for s in range(PACK_ROWS)], axis=1)
    lo = lax.bitcast_convert_type(lax.shift_left(pk, jnp.uint32(16)), F32)
    hi = lax.bitcast_convert_type(pk & jnp.uint32(0xFFFF0000), F32)
    return lo, hi


def _proj_ln_kernel(*refs, n_in):
    a_refs, w_refs = refs[:n_in], refs[n_in:2 * n_in]
    x_ref, g_ref, b_ref, wr_ref, br_ref, o_ref, opk_ref, route_ref = refs[2 * n_in:]
    tm = x_ref.shape[0]
    sub = tm // PROJ_SPLIT
    for r in range(PROJ_SPLIT):
        rows = pl.ds(r * sub, sub)
        y = jnp.dot(a_refs[0][rows, :], w_refs[0][...], preferred_element_type=F32)
        for a_ref, w_ref in zip(a_refs[1:], w_refs[1:]):
            y = y + jnp.dot(a_ref[rows, :], w_ref[...], preferred_element_type=F32)
        z = DEEPNORM_ALPHA * x_ref[rows, :] + y
        xn = _layer_norm_rows(z, g_ref[...], b_ref[...])
        o_ref[rows, :] = xn
        _pack_tokens(xn, opk_ref.at[pl.ds(r * sub * PACK_ROWS, sub * PACK_ROWS)])
        route_ref[rows, :] = _route_rows(xn, wr_ref, br_ref)


def proj_residual_ln(acts, weights, x, g, b, w_route, b_route, *, tm):
    n, d = x.shape
    n_in = len(acts)
    const = lambda i: (0, 0)
    in_specs = [pl.BlockSpec((tm, a.shape[1]), lambda i: (i, 0)) for a in acts]
    in_specs += [pl.BlockSpec(w.shape, const, pipeline_mode=pl.Buffered(1)) for w in weights]
    in_specs += [pl.BlockSpec((tm, d), lambda i: (i, 0)), pl.BlockSpec((1, d), const), pl.BlockSpec((1, d), const),
                 pl.BlockSpec((d, LANES), const), pl.BlockSpec((1, LANES), const)]
    return pl.pallas_call(
        functools.partial(_proj_ln_kernel, n_in=n_in),
        grid=(n // tm,),
        in_specs=in_specs,
        out_specs=[pl.BlockSpec((tm, d), lambda i: (i, 0)), pl.BlockSpec((tm * PACK_ROWS, LANES), lambda i: (i, 0)),
                   pl.BlockSpec((tm, LANES), lambda i: (i, 0))],
        out_shape=[jax.ShapeDtypeStruct((n, d), F32), jax.ShapeDtypeStruct((n * PACK_ROWS, LANES), U32),
                   jax.ShapeDtypeStruct((n, LANES), F32)],
        compiler_params=_cparams(("parallel",)),
        name="proj_residual_ln",
    )(*acts, *weights, x, g.reshape(1, d), b.reshape(1, d), w_route, b_route)


def _rope_lanes(x, c_ref, s_up_ref, s_dn_ref):
    half = ROPE_DIM // 2
    return (x * c_ref[...] + pltpu.roll(x, half, axis=1) * s_up_ref[...]
            + pltpu.roll(x, LANES - half, axis=1) * s_dn_ref[...])


def _rms_rows(x, g):
    return x * lax.rsqrt(jnp.mean(x * x, axis=-1, keepdims=True) + RMS_EPS) * g


def _mla_in_kernel(x_ref, w_ref, gq_ref, gkv_ref, c_ref, su_ref, sd_ref, cq_ref, ckv_ref, kr_ref):
    h = jnp.dot(x_ref[...].astype(BF16), w_ref[...], preferred_element_type=F32)
    cq_ref[...] = _rms_rows(h[:, :Q_LORA], gq_ref[...]).astype(cq_ref.dtype)
    ckv_ref[...] = _rms_rows(h[:, Q_LORA:Q_LORA + KV_LORA], gkv_ref[...]).astype(ckv_ref.dtype)
    kr_ref[...] = _rope_lanes(h[:, Q_LORA + KV_LORA:], c_ref, su_ref, sd_ref).astype(kr_ref.dtype)


def mla_in_proj(x, w_in_pad, g_q, g_kv, rope_tabs, *, seq, tm):
    n, d = x.shape
    wn = w_in_pad.shape[1]
    npos = seq // tm
    tab = pl.BlockSpec((tm, LANES), lambda i: (i % npos, 0))
    return pl.pallas_call(
        _mla_in_kernel,
        grid=(n // tm,),
        in_specs=[pl.BlockSpec((tm, d), lambda i: (i, 0)), pl.BlockSpec((d, wn), lambda i: (0, 0)),
                  pl.BlockSpec((1, Q_LORA), lambda i: (0, 0)), pl.BlockSpec((1, KV_LORA), lambda i: (0, 0)),
                  tab, tab, tab],
        out_specs=[pl.BlockSpec((tm, Q_LORA), lambda i: (i, 0)), pl.BlockSpec((tm, KV_LORA), lambda i: (i, 0)),
                   pl.BlockSpec((tm, LANES), lambda i: (i, 0))],
        out_shape=[jax.ShapeDtypeStruct((n, Q_LORA), BF16), jax.ShapeDtypeStruct((n, KV_LORA), BF16),
                   jax.ShapeDtypeStruct((n, LANES), BF16)],
        compiler_params=_cparams(("parallel",)),
        name="mla_in_proj",
    )(x, w_in_pad, g_q.reshape(1, -1), g_kv.reshape(1, -1), *rope_tabs)


def _mla_q_kernel(cq_ref, w_ref, c_ref, su_ref, sd_ref, o_ref, *, heads):
    cq = cq_ref[...]
    for hh in range(heads):
        base = hh * 2 * LANES
        q = jnp.dot(cq, w_ref[:, base:base + 2 * LANES], preferred_element_type=F32)
        o_ref[:, base:base + LANES] = q[:, :LANES].astype(o_ref.dtype)
        o_ref[:, base + LANES:base + 2 * LANES] = _rope_lanes(q[:, LANES:], c_ref, su_ref, sd_ref).astype(o_ref.dtype)


def mla_q_proj(cq, w_uq_cat, rope_tabs, *, seq, tm, heads_per_step):
    n = cq.shape[0]
    wn = w_uq_cat.shape[1]
    tn = heads_per_step * 2 * LANES
    npos = seq // tm
    tab = pl.BlockSpec((tm, LANES), lambda i, j: (i % npos, 0))
    return pl.pallas_call(
        functools.partial(_mla_q_kernel, heads=heads_per_step),
        grid=(n // tm, wn // tn),
        in_specs=[pl.BlockSpec((tm, Q_LORA), lambda i, j: (i, 0)), pl.BlockSpec((Q_LORA, tn), lambda i, j: (0, j)),
                  tab, tab, tab],
        out_specs=pl.BlockSpec((tm, tn), lambda i, j: (i, j)),
        out_shape=jax.ShapeDtypeStruct((n, wn), BF16),
        compiler_params=_cparams(("parallel", "parallel")),
        name="mla_q_proj",
    )(cq, w_uq_cat, *rope_tabs)


def _mla_attn_kernel(q_ref, kn_ref, kr_ref, v_ref, o_ref, m_sc, acc_sc, *, tq, tk, heads):
    qi = pl.program_id(2)
    n_full = qi * (tq // tk)
    m_sc[...] = jnp.full_like(m_sc, NEG_INF)
    acc_sc[...] = jnp.zeros_like(acc_sc)
    ones = jnp.ones((tk, LANES), BF16)
    row_chunk = jnp.right_shift(lax.broadcasted_iota(I32, (tq, 1), 0), CHUNK_SHIFT)
    col_chunk = jnp.right_shift(lax.broadcasted_iota(I32, (1, tk), 1), CHUNK_SHIFT)

    def step(kj, diag):
        off = pl.multiple_of(kj * tk, tk)
        kr = kr_ref[pl.ds(off, tk), :]
        for hh in range(heads):
            k = jnp.concatenate([kn_ref[pl.ds(off, tk), hh * NOPE_DIM:(hh + 1) * NOPE_DIM], kr], axis=1)
            s = lax.dot_general(q_ref[:, hh * 2 * LANES:(hh + 1) * 2 * LANES], k, (((1,), (1,)), ((), ())),
                                preferred_element_type=F32)
            if diag is not None:
                s = jnp.where(col_chunk + diag * (tk // CHUNK) <= row_chunk, s, NEG_INF)
            m_prev = m_sc[hh]
            m_new = jnp.maximum(m_prev, jnp.max(s, axis=1, keepdims=True))
            alpha = jnp.exp2(m_prev - m_new)
            p = jnp.exp2(s - jnp.tile(m_new, (1, tk // LANES)))
            va = jnp.concatenate([v_ref[pl.ds(off, tk), hh * V_DIM:(hh + 1) * V_DIM], ones], axis=1)
            pv = jnp.dot(p.astype(BF16), va, preferred_element_type=F32)
            acc_sc[hh] = jnp.tile(alpha, (1, 2)) * acc_sc[hh] + pv
            m_sc[hh] = m_new

    def two_full_steps(i, c):
        step(2 * i, None)
        step(2 * i + 1, None)
        return c

    lax.fori_loop(0, n_full // 2, two_full_steps, 0)

    @pl.when(n_full % 2 == 1)
    def _():
        step(n_full - 1, None)

    for dj in range(tq // tk):
        step(n_full + dj, dj)
    for hh in range(heads):
        acc = acc_sc[hh]
        o_ref[:, hh * V_DIM:(hh + 1) * V_DIM] = (acc[:, :V_DIM] / acc[:, V_DIM:]).astype(o_ref.dtype)


def mla_attention(q_cat, kv, k_rope, *, batch, seq, tq, tk, heads):
    n = q_cat.shape[0]
    nq = seq // tq
    n_hb = MLA_HEADS // heads
    return pl.pallas_call(
        functools.partial(_mla_attn_kernel, tq=tq, tk=tk, heads=heads),
        grid=(batch, n_hb, nq),
        in_specs=[
            pl.BlockSpec((tq, heads * 2 * LANES), lambda b, h, qi: (b * nq + qi, h)),
            pl.BlockSpec((seq, heads * NOPE_DIM), lambda b, h, qi: (b, h), pipeline_mode=pl.Buffered(1)),
            pl.BlockSpec((seq, LANES), lambda b, h, qi: (b, 0), pipeline_mode=pl.Buffered(1)),
            pl.BlockSpec((seq, heads * V_DIM), lambda b, h, qi: (b, n_hb + h), pipeline_mode=pl.Buffered(1)),
        ],
        out_specs=pl.BlockSpec((tq, heads * V_DIM), lambda b, h, qi: (b * nq + qi, h)),
        out_shape=jax.ShapeDtypeStruct((n, MLA_HEADS * V_DIM), BF16),
        scratch_shapes=[pltpu.VMEM((heads, tq, LANES), F32), pltpu.VMEM((heads, tq, 2 * LANES), F32)],
        compiler_params=_cparams(("parallel", "parallel", "arbitrary")),
        name="mla_attention",
    )(q_cat, kv, k_rope, kv)


def rope_tables(seq):
    half = ROPE_DIM // 2
    freq = ROPE_THETA ** (-jnp.arange(half, dtype=F32) / half)
    ang = jnp.arange(seq, dtype=F32)[:, None] * freq[None, :]
    cos, sin = jnp.cos(ang), jnp.sin(ang)
    z = jnp.zeros_like(cos)
    c = jnp.concatenate([cos, cos, z, z], axis=1)
    s_up = jnp.concatenate([z, sin, z, z], axis=1)
    s_dn = jnp.concatenate([-sin, z, z, z], axis=1)
    return c, s_up, s_dn


def _route_rows(xn, w_ref, b_ref):
    logits = jnp.dot(xn.astype(BF16), w_ref[...], preferred_element_type=F32) + b_ref[...]
    tm = logits.shape[0]
    lane = lax.broadcasted_iota(I32, (tm, LANES), 1)
    is_group = lane < N_GROUPS
    lg = jnp.where(is_group, logits, NEG_INF)
    mg = jnp.max(lg, axis=1, keepdims=True)
    g_sel = jnp.min(jnp.where(lg == mg, lane, LANES), axis=1, keepdims=True)
    p_group = 1.0 / jnp.sum(jnp.where(is_group, jnp.exp(lg - mg), 0.0), axis=1, keepdims=True)
    e_lane = lane - N_GROUPS
    in_sel = (e_lane >= 0) & (e_lane < N_EXPERTS) & (jnp.right_shift(e_lane, GROUP_SHIFT) == g_sel)
    le = jnp.where(in_sel, logits, NEG_INF)
    v1 = jnp.max(le, axis=1, keepdims=True)
    i1 = jnp.min(jnp.where(le == v1, lane, LANES), axis=1, keepdims=True)
    le2 = jnp.where(lane == i1, NEG_INF, le)
    v2 = jnp.max(le2, axis=1, keepdims=True)
    i2 = jnp.min(jnp.where(le2 == v2, lane, LANES), axis=1, keepdims=True)
    t = jnp.exp(v2 - v1)
    p1 = 1.0 / (1.0 + t)
    p2 = t / (1.0 + t)
    out = jnp.where(lane == 0, (i1 - N_GROUPS).astype(F32), 0.0)
    out = jnp.where(lane == 1, (i2 - N_GROUPS).astype(F32), out)
    out = jnp.where(lane == 2, p_group * p1, out)
    out = jnp.where(lane == 3, p_group * p2, out)
    return out


def _token_copy(src, src_tok, dst, dst_tok, sem):
    def first_row(tok):
        row = tok * PACK_ROWS
        return row if isinstance(row, int) else pl.multiple_of(row, PACK_ROWS)

    return pltpu.make_async_copy(src.at[pl.ds(first_row(src_tok), PACK_ROWS)],
                                 dst.at[pl.ds(first_row(dst_tok), PACK_ROWS)], sem)


def _dispatch_kernel(dest_ref, pad0_ref, padn_ref, nu_ref, x_ref, xs_ref, sem, sem_fill, *, tm, tb, max_blocks):
    base = pl.program_id(0) * tm * 2

    @pl.when(pl.program_id(0) == 0)
    def _():
        def fill(first_slot, n_slots):
            dst = xs_ref.at[pl.ds(pl.multiple_of(first_slot * PACK_ROWS, PACK_ROWS), n_slots * PACK_ROWS)]
            return pltpu.make_async_copy(x_ref.at[pl.ds(0, n_slots * PACK_ROWS)], dst, sem_fill)

        def pad_fills(ex):
            slot, left = pad0_ref[ex], padn_ref[ex]
            out = []
            for bit in reversed(range(tb.bit_length() - 1)):
                take = (left & (1 << bit)) != 0
                out.append((take, fill(slot, 1 << bit)))
                slot = slot + jnp.where(take, 1 << bit, 0)
            return out

        def for_pads(action):
            def body(ex, c):
                for take, cp in pad_fills(ex):
                    pl.when(take)(functools.partial(action, cp))
                return c
            lax.fori_loop(0, N_EXPERTS, body, 0)

        def for_tail(action):
            def body(g, c):
                action(fill(g * tb, tb))
                return c
            lax.fori_loop(nu_ref[0], max_blocks, body, 0)

        for_pads(lambda cp: cp.start())
        for_tail(lambda cp: cp.start())
        for_pads(lambda cp: cp.wait())
        for_tail(lambda cp: cp.wait())

    copies = [_token_copy(x_ref, a // 2, xs_ref, dest_ref[base + a], sem) for a in range(2 * tm)]
    for a, cp in enumerate(copies):
        cp.start(priority=a % 2)
    for cp in copies:
        cp.wait()


def moe_dispatch(x_pk, dest, pad_start, pad_len, n_used, max_blocks, *, tm, tb):
    n = x_pk.shape[0] // PACK_ROWS
    assert tm >= tb
    return pl.pallas_call(
        functools.partial(_dispatch_kernel, tm=tm, tb=tb, max_blocks=max_blocks),
        grid_spec=pltpu.PrefetchScalarGridSpec(
            num_scalar_prefetch=4,
            grid=(n // tm,),
            in_specs=[pl.BlockSpec((tm * PACK_ROWS, LANES), lambda i, *_: (i, 0))],
            out_specs=pl.BlockSpec(memory_space=pl.ANY),
            scratch_shapes=[pltpu.SemaphoreType.DMA(()), pltpu.SemaphoreType.DMA(())],
        ),
        out_shape=jax.ShapeDtypeStruct((max_blocks * tb * PACK_ROWS, LANES), U32),
        compiler_params=_cparams(("arbitrary",)),
        name="moe_dispatch",
    )(dest, pad_start, pad_len, n_used, x_pk)


def _expert_kernel(b0_ref, nb_ref, nu_ref, xs_ref, wg_hbm, wu_hbm, wd_hbm, ys_ref,
                   xbuf, ybuf, wg_buf, wu_buf, wd_buf, sem_in, sem_out, sem_w, *, tb, layer):
    e = pl.program_id(0)
    n_experts = pl.num_programs(0)
    b0, nb, n_used = b0_ref[e], nb_ref[e], nu_ref[0]
    rows = tb * PACK_ROWS

    def weight_copies(ex):
        slot = lax.rem(ex, WEIGHT_RING)
        copies = []
        for k, (hbm, buf) in enumerate(((wg_hbm, wg_buf), (wu_hbm, wu_buf), (wd_hbm, wd_buf))):
            half = buf.shape[1] // 2
            for part in range(2):
                rows_ = pl.ds(part * half, half)
                copies.append(pltpu.make_async_copy(hbm.at[layer, ex, rows_], buf.at[slot, rows_],
                                                    sem_w.at[2 * k + part, slot]))
        return copies

    def in_copy(g):
        src = xs_ref.at[pl.ds(pl.multiple_of(g * rows, rows), rows)]
        return pltpu.make_async_copy(src, xbuf.at[g % 2], sem_in.at[g % 2])

    def out_copy(g):
        dst = ys_ref.at[pl.ds(pl.multiple_of(g * rows, rows), rows)]
        return pltpu.make_async_copy(ybuf.at[g % 2], dst, sem_out.at[g % 2])

    @pl.when(e == 0)
    def _():
        in_copy(0).start()
        for ex in range(WEIGHT_RING - 1):
            for idx, cp in enumerate(weight_copies(ex)):
                cp.start(priority=idx % 2)

    @pl.when(e + WEIGHT_RING - 1 < n_experts)
    def _():
        for idx, cp in enumerate(weight_copies(e + WEIGHT_RING - 1)):
            cp.start(priority=idx % 2)

    for cp in weight_copies(e):
        cp.wait()
    slot = lax.rem(e, WEIGHT_RING)

    @pl.when(nb > 0)
    def _():
        def block(g, c):
            @pl.when(g + 1 < n_used)
            def _():
                in_copy(g + 1).start()

            in_copy(g).wait()

            @pl.when(g >= 2)
            def _():
                out_copy(g - 2).wait()

            lo, hi = _unpack_tokens(xbuf.at[g % 2], 0, tb)
            xb = jnp.concatenate([lo.astype(BF16), hi.astype(BF16)], axis=1)
            hg = jnp.dot(xb, wg_buf[slot].astype(BF16), preferred_element_type=F32)
            hu = jnp.dot(xb, wu_buf[slot].astype(BF16), preferred_element_type=F32)
            hb = (hg * jax.nn.sigmoid(hg) * hu).astype(BF16)
            _pack_tokens(jnp.dot(hb, wd_buf[slot].astype(BF16), preferred_element_type=F32), ybuf.at[g % 2])
            out_copy(g).start(priority=1)
            return c

        lax.fori_loop(b0, b0 + nb, block, 0)

    @pl.when(e == n_experts - 1)
    def _():
        @pl.when(n_used >= 2)
        def _():
            out_copy(n_used - 2).wait()

        out_copy(n_used - 1).wait()


def moe_experts(xs, first_block, n_blocks, n_used, w_gate, w_up, w_down, *, layer, tb):
    _, n_experts, d, hdim = w_gate.shape
    assert n_experts >= WEIGHT_RING
    ring = pltpu.VMEM((2, tb * PACK_ROWS, LANES), U32)
    any_space = pl.BlockSpec(memory_space=pl.ANY)
    return pl.pallas_call(
        functools.partial(_expert_kernel, tb=tb, layer=layer),
        grid_spec=pltpu.PrefetchScalarGridSpec(
            num_scalar_prefetch=3,
            grid=(n_experts,),
            in_specs=[any_space, any_space, any_space, any_space],
            out_specs=any_space,
            scratch_shapes=[ring, ring, pltpu.VMEM((WEIGHT_RING, d, hdim), F32), pltpu.VMEM((WEIGHT_RING, d, hdim), F32),
                            pltpu.VMEM((WEIGHT_RING, hdim, d), F32), pltpu.SemaphoreType.DMA((2,)),
                            pltpu.SemaphoreType.DMA((2,)), pltpu.SemaphoreType.DMA((6, WEIGHT_RING))],
        ),
        out_shape=jax.ShapeDtypeStruct(xs.shape, xs.dtype),
        input_output_aliases={3: 0},
        compiler_params=_cparams(("arbitrary",)),
        name="moe_experts",
    )(first_block, n_blocks, n_used, xs, w_gate, w_up, w_down)


def _combine_ln_kernel(dest_ref, x_ref, route_ref, g_ref, b_ref, ys_ref, o_ref, buf, sem, *, tm):
    i = pl.program_id(0)

    def gathers(step):
        slot = step % 2
        return [_token_copy(ys_ref, dest_ref[step * 2 * tm + a], buf.at[slot], (a % 2) * tm + a // 2, sem.at[slot])
                for a in range(2 * tm)]

    def start(step):
        for a, cp in enumerate(gathers(step)):
            cp.start(priority=a % 2)

    @pl.when(i == 0)
    def _():
        start(0)

    @pl.when(i + 1 < pl.num_programs(0))
    def _():
        start(i + 1)

    for cp in gathers(i):
        cp.wait()
    cur = buf.at[i % 2]
    g1 = route_ref[:, 2:3]
    g2 = route_ref[:, 3:4]
    lo1, hi1 = _unpack_tokens(cur, 0, tm)
    lo2, hi2 = _unpack_tokens(cur, tm * PACK_ROWS, tm)
    f = jnp.concatenate([lo1 * g1 + lo2 * g2, hi1 * g1 + hi2 * g2], axis=1)
    z = DEEPNORM_ALPHA * x_ref[...] + f
    o_ref[...] = _layer_norm_rows(z, g_ref[...], b_ref[...])


def moe_combine_ln(x, route, dest, ys, g, b, *, tm):
    n, d = x.shape
    return pl.pallas_call(
        functools.partial(_combine_ln_kernel, tm=tm),
        grid_spec=pltpu.PrefetchScalarGridSpec(
            num_scalar_prefetch=1,
            grid=(n // tm,),
            in_specs=[pl.BlockSpec((tm, d), lambda i, dest: (i, 0)), pl.BlockSpec((tm, LANES), lambda i, dest: (i, 0)),
                      pl.BlockSpec((1, d), lambda i, dest: (0, 0)), pl.BlockSpec((1, d), lambda i, dest: (0, 0)),
                      pl.BlockSpec(memory_space=pl.ANY)],
            out_specs=pl.BlockSpec((tm, d), lambda i, dest: (i, 0)),
            scratch_shapes=[pltpu.VMEM((2, 2 * tm * PACK_ROWS, LANES), U32), pltpu.SemaphoreType.DMA((2,))],
        ),
        out_shape=jax.ShapeDtypeStruct((n, d), F32),
        compiler_params=_cparams(("arbitrary",)),
        name="moe_combine_ln",
    )(dest, x, route, g.reshape(1, d), b.reshape(1, d), ys)


def routing_tables(route, *, tb):
    n = route.shape[0]
    e_flat = route[:, :2].astype(I32).reshape(-1)
    onehot = (e_flat[:, None] == jnp.arange(N_EXPERTS, dtype=I32)[None, :]).astype(I32)
    csum = jnp.cumsum(onehot, axis=0)
    rank = jnp.sum(csum * onehot, axis=1) - 1
    counts = csum[-1]
    padded = ((counts + tb - 1) // tb) * tb
    pends = jnp.cumsum(padded)
    pstarts = pends - padded
    dest = jnp.sum(onehot * pstarts[None, :], axis=1) + rank
    n_used = (pends[-1] // tb).astype(I32).reshape(1)
    tables = dict(dest=dest, first_block=pstarts // tb, n_blocks=padded // tb, pad_start=pstarts + counts,
                  pad_len=padded - counts, n_used=n_used)
    return {k: v.astype(I32) for k, v in tables.items()}


def route_weights(w_group, b_group, w_expert, b_expert):
    d = w_group.shape[0]
    n_pad = LANES - N_GROUPS - N_EXPERTS
    w = jnp.concatenate([w_group, w_expert, jnp.zeros((d, n_pad), F32)], axis=1).astype(BF16)
    b = jnp.concatenate([b_group, b_expert, jnp.zeros((n_pad,), F32)]).reshape(1, LANES)
    return w, b


def hier_moe_ln(x, x_pk, route, w_gate, w_up, w_down, g, b, *, layer, tb=256, tm=256):
    t = routing_tables(route, tb=tb)
    max_blocks = (2 * x.shape[0]) // tb + N_EXPERTS
    xs = moe_dispatch(x_pk, t["dest"], t["pad_start"], t["pad_len"], t["n_used"], max_blocks, tm=2 * tm, tb=tb)
    ys = moe_experts(xs, t["first_block"], t["n_blocks"], t["n_used"], w_gate, w_up, w_down, layer=layer, tb=tb)
    return moe_combine_ln(x, route, t["dest"], ys, g, b, tm=tm)


def pool_band_layer(x, w_in, w_pool, pool_scale, rel_bias, w_o, g, b, w_route, b_route, *, batch, seq):
    qscale = jnp.concatenate([jnp.ones((POOL_WIDTH,), F32), jnp.full((ATT_WIDTH,), ATT_HEAD_DIM ** -0.5 * LOG2_E, F32),
                              jnp.ones((2 * ATT_WIDTH,), F32)])
    h = matmul(x, (w_in * qscale).astype(BF16), tm=512, tn=1024, out_dtype=BF16)
    a_out = pool_mixer(h, w_pool.astype(BF16), pool_scale.reshape(1, -1), seq=seq, tm=512)
    b_out = band_attention(h, band_bias_table(rel_bias), batch=batch, seq=seq, tq=512, pairs=4)
    w_o = w_o.astype(BF16)
    return proj_residual_ln([a_out, b_out], [w_o[:POOL_WIDTH], w_o[POOL_WIDTH:]], x, g, b, w_route, b_route, tm=512)


def mla_layer(x, w_in, g_q, g_kv, w_uq, w_uk, w_uv, w_o, g, b, w_route, b_route, *, batch, seq):
    d = x.shape[1]
    tabs = rope_tables(seq)
    w_in_pad = jnp.concatenate([w_in, jnp.zeros((d, LANES - ROPE_DIM), F32)], axis=1).astype(BF16)
    cq, ckv, k_rope = mla_in_proj(x, w_in_pad, g_q, g_kv, tabs, seq=seq, tm=512)
    w_uq_h = w_uq.reshape(Q_LORA, MLA_HEADS, NOPE_DIM + ROPE_DIM)
    w_uq_cat = jnp.concatenate([w_uq_h, jnp.zeros((Q_LORA, MLA_HEADS, 2 * LANES - NOPE_DIM - ROPE_DIM), F32)], axis=2)
    q_scale = (NOPE_DIM + ROPE_DIM) ** -0.5 * LOG2_E
    w_uq_cat = (w_uq_cat.reshape(Q_LORA, MLA_HEADS * 2 * LANES) * q_scale).astype(BF16)
    q_cat = mla_q_proj(cq, w_uq_cat, tabs, seq=seq, tm=512, heads_per_step=MLA_HEADS)
    w_kv = jnp.concatenate([w_uk, w_uv], axis=1).astype(BF16)
    kv = matmul(ckv, w_kv, tm=512, tn=w_kv.shape[1], out_dtype=BF16)
    o = mla_attention(q_cat, kv, k_rope, batch=batch, seq=seq, tq=512, tk=512, heads=8)
    return proj_residual_ln([o], [w_o.astype(BF16)], x, g, b, w_route, b_route, tm=512)


def kernel(x, ab_w_in, ab_w_pool, ab_pool_scale, ab_rel_bias, ab_w_o, mla_w_in, mla_g_q, mla_g_kv, mla_w_uq, mla_w_uk, mla_w_uv, mla_w_o, ln1_g, ln1_b, moe_w_group, moe_b_group, moe_w_expert, moe_b_expert, moe_w_gate, moe_w_up, moe_w_down, ln2_g, ln2_b):
    batch, seq, d = x.shape
    x = x.reshape(batch * seq, d)
    for i in range(DEPTH):
        j = i // 2
        w_route, b_route = route_weights(moe_w_group[i], moe_b_group[i], moe_w_expert[i], moe_b_expert[i])
        if i % 2 == 0:
            x, x_pk, route = pool_band_layer(x, ab_w_in[j], ab_w_pool[j], ab_pool_scale[j], ab_rel_bias[j], ab_w_o[j],
                                             ln1_g[i], ln1_b[i], w_route, b_route, batch=batch, seq=seq)
        else:
            x, x_pk, route = mla_layer(x, mla_w_in[j], mla_g_q[j], mla_g_kv[j], mla_w_uq[j], mla_w_uk[j], mla_w_uv[j],
                                       mla_w_o[j], ln1_g[i], ln1_b[i], w_route, b_route, batch=batch, seq=seq)
        x = hier_moe_ln(x, x_pk, route, moe_w_gate, moe_w_up, moe_w_down, ln2_g[i], ln2_b[i], layer=i)
    return x.reshape(batch, seq, d)
```

```python
import functools

import jax
import jax.numpy as jnp
from jax import lax
from jax.experimental import pallas as pl
from jax.experimental.pallas import tpu as pltpu

F32 = jnp.float32
BF16 = jnp.bfloat16
I32 = jnp.int32
U32 = jnp.uint32

D_MODEL = 2048
DEPTH = 4
CHUNK = 64
CHUNK_SHIFT = 6
GROUP_SHIFT = 3
POOL_WINDOWS = (2, 4, 8, 16)
POOL_GROUP_DIM = 256
POOL_WIDTH = 1024
ATT_WIDTH = 1024
ATT_HEADS = 16
ATT_HEAD_DIM = 64
LEFT_CHUNKS = 8
REL_MAX = 128
MLA_HEADS = 16
Q_LORA = 512
KV_LORA = 512
NOPE_DIM = 128
ROPE_DIM = 64
V_DIM = 128
ROPE_THETA = 10000.0
N_GROUPS = 8
EXPERTS_PER_GROUP = 8
N_EXPERTS = 64
EXPERT_HIDDEN = 512
DEEPNORM_ALPHA = (2 * DEPTH) ** 0.25
LN_EPS = 1e-5
RMS_EPS = 1e-6

LANES = 128
PACK_ROWS = D_MODEL // 2 // LANES
PROJ_SPLIT = 2
KV_UNROLL = 2
WEIGHT_RING = 4
POOL_HALO = 16
BAND_SUB = 128
BAND_KEYS = BAND_SUB + LEFT_CHUNKS * CHUNK
NEG_INF = float("-inf")
LOG2_E = 1.4426950408889634
VMEM_LIMIT = 56 * 1024 * 1024


def _cparams(sem, vmem=VMEM_LIMIT):
    return pltpu.CompilerParams(dimension_semantics=sem, vmem_limit_bytes=vmem)


def _mm_kernel(x_ref, w_ref, o_ref, *, tn):
    xb = x_ref[...].astype(BF16)
    for c in range(o_ref.shape[1] // tn):
        cols = slice(c * tn, (c + 1) * tn)
        o_ref[:, cols] = jnp.dot(xb, w_ref[:, cols], preferred_element_type=F32).astype(o_ref.dtype)


def matmul(x, w, *, tm, tn, out_dtype):
    m, k = x.shape
    n = w.shape[1]
    return pl.pallas_call(
        functools.partial(_mm_kernel, tn=tn),
        grid=(m // tm,),
        in_specs=[pl.BlockSpec((tm, k), lambda i: (i, 0)),
                  pl.BlockSpec((k, n), lambda i: (0, 0), pipeline_mode=pl.Buffered(1))],
        out_specs=pl.BlockSpec((tm, n), lambda i: (i, 0)),
        out_shape=jax.ShapeDtypeStruct((m, n), out_dtype),
        compiler_params=_cparams(("parallel",)),
        name="matmul",
    )(x, w)


def _pool_kernel(u_ref, halo_ref, w_ref, scale_ref, o_ref, *, tm, seq):
    i = pl.program_id(0)
    pos0 = (i * tm) % seq
    u = u_ref[...].astype(F32)
    halo = jnp.where(pos0 == 0, 0.0, halo_ref[...].astype(F32))
    ue = jnp.concatenate([halo, u], axis=0)
    pos = pos0 + lax.broadcasted_iota(I32, (tm, 1), 0)
    for g, win in enumerate(POOL_WINDOWS):
        cols = slice(g * POOL_GROUP_DIM, (g + 1) * POOL_GROUP_DIM)
        s = ue[:, cols]
        shift = 1
        while shift < win:
            s = s + pltpu.roll(s, shift, axis=0)
            shift *= 2
        wsum = s[POOL_HALO:, :]
        count = jnp.minimum(pos + 1, win).astype(F32)
        d = wsum / count - u[:, cols]
        y = jnp.dot(d.astype(BF16), w_ref[g], preferred_element_type=F32) * scale_ref[:, cols]
        o_ref[:, cols] = y.astype(o_ref.dtype)


def pool_mixer(h, w_pool, pool_scale, *, seq, tm):
    n = h.shape[0]
    hb = tm // POOL_HALO
    return pl.pallas_call(
        functools.partial(_pool_kernel, tm=tm, seq=seq),
        grid=(n // tm,),
        in_specs=[
            pl.BlockSpec((tm, POOL_WIDTH), lambda i: (i, 0)),
            pl.BlockSpec((POOL_HALO, POOL_WIDTH), lambda i: (jnp.maximum(i * hb - 1, 0), 0)),
            pl.BlockSpec((len(POOL_WINDOWS), POOL_GROUP_DIM, POOL_GROUP_DIM), lambda i: (0, 0, 0)),
            pl.BlockSpec((1, POOL_WIDTH), lambda i: (0, 0)),
        ],
        out_specs=pl.BlockSpec((tm, POOL_WIDTH), lambda i: (i, 0)),
        out_shape=jax.ShapeDtypeStruct((n, POOL_WIDTH), BF16),
        compiler_params=_cparams(("parallel",)),
        name="pool_mixer",
    )(h, h, w_pool, pool_scale)


def _band_kernel(q_ref, kp_ref, kc_ref, vp_ref, vc_ref, bias_ref, o_ref, *, tq, pairs):
    i = pl.program_id(1)
    lane = lax.broadcasted_iota(I32, (1, LANES), 1)
    first_head = lane < ATT_HEAD_DIM
    col = lax.broadcasted_iota(I32, (1, BAND_KEYS), 1)
    left = LEFT_CHUNKS * CHUNK
    ones = jnp.ones((BAND_KEYS, LANES), BF16)
    for pr in range(pairs):
        lanes = slice(pr * LANES, (pr + 1) * LANES)
        kk = jnp.concatenate([kp_ref[:, lanes], kc_ref[:, lanes]], axis=0)
        vv = jnp.concatenate([vp_ref[:, lanes], vc_ref[:, lanes]], axis=0)
        for j in range(tq // BAND_SUB):
            rows = slice(j * BAND_SUB, (j + 1) * BAND_SUB)
            qs = q_ref[rows, lanes]
            off = tq - left + j * BAND_SUB
            kw = kk[off:off + BAND_KEYS, :]
            vw = jnp.concatenate([vv[off:off + BAND_KEYS, :], ones], axis=1)
            valid = (i * tq + j * BAND_SUB - left + col) >= 0
            outs = []
            for hh in range(2):
                qm = jnp.where(first_head if hh == 0 else jnp.logical_not(first_head), qs, jnp.zeros_like(qs))
                s = lax.dot_general(qm, kw, (((1,), (1,)), ((), ())), preferred_element_type=F32)
                s = jnp.where(valid, s + bias_ref[2 * pr + hh], NEG_INF)
                m = jnp.max(s, axis=1, keepdims=True)
                p = jnp.exp2(s - m)
                o = jnp.dot(p.astype(BF16), vw, preferred_element_type=F32)
                outs.append(o[:, :LANES] / o[:, LANES:])
            o_ref[rows, lanes] = jnp.where(first_head, outs[0], outs[1]).astype(o_ref.dtype)


def band_bias_table(rel_bias):
    heads, rel_size = rel_bias.shape
    r = jnp.arange(BAND_SUB, dtype=I32)[:, None]
    c = jnp.arange(BAND_KEYS, dtype=I32)[None, :]
    qc, kc = r // CHUNK, c // CHUNK
    allowed = (kc >= qc) & (kc <= qc + LEFT_CHUNKS)
    n_lo = BAND_SUB - CHUNK
    n_hi = BAND_KEYS - 1 - REL_MAX
    by_dist = jnp.concatenate([jnp.repeat(rel_bias[:, :1], n_lo, axis=1), rel_bias,
                               jnp.repeat(rel_bias[:, -1:], n_hi, axis=1)], axis=1).astype(F32)
    period = BAND_SUB + BAND_KEYS - 1
    w = jnp.concatenate([by_dist[:, :BAND_KEYS][:, ::-1], by_dist[:, BAND_KEYS:][:, ::-1]], axis=1)
    rows = jnp.tile(w, (1, BAND_SUB))[:, :BAND_SUB * (period - 1)].reshape(heads, BAND_SUB, period - 1)
    return jnp.where(allowed[None], rows[:, :, :BAND_KEYS] * LOG2_E, NEG_INF)


def band_attention(h, bias_tab, *, batch, seq, tq, pairs):
    n = h.shape[0]
    nq = seq // tq
    width = pairs * LANES
    qcol = POOL_WIDTH // width
    kcol = qcol + ATT_WIDTH // width
    vcol = kcol + ATT_WIDTH // width
    cur = lambda c0: (lambda b, i, p: (b * nq + i, c0 + p))
    prev = lambda c0: (lambda b, i, p: (b * nq + jnp.maximum(i - 1, 0), c0 + p))
    blk = (tq, width)
    return pl.pallas_call(
        functools.partial(_band_kernel, tq=tq, pairs=pairs),
        grid=(batch, nq, ATT_HEADS // (2 * pairs)),
        in_specs=[
            pl.BlockSpec(blk, cur(qcol)),
            pl.BlockSpec(blk, prev(kcol)), pl.BlockSpec(blk, cur(kcol)),
            pl.BlockSpec(blk, prev(vcol)), pl.BlockSpec(blk, cur(vcol)),
            pl.BlockSpec((2 * pairs, BAND_SUB, BAND_KEYS), lambda b, i, p: (p, 0, 0)),
        ],
        out_specs=pl.BlockSpec(blk, lambda b, i, p: (b * nq + i, p)),
        out_shape=jax.ShapeDtypeStruct((n, ATT_WIDTH), BF16),
        compiler_params=_cparams(("parallel", "parallel", "parallel")),
        name="band_attention",
    )(h, h, h, h, h, bias_tab)


def _layer_norm_rows(z, g, b):
    mu = jnp.mean(z, axis=-1, keepdims=True)
    zc = z - mu
    var = jnp.mean(zc * zc, axis=-1, keepdims=True)
    return zc * lax.rsqrt(var + LN_EPS) * g + b


def _pack_tokens(z, ref):
    tm, d = z.shape
    lo = lax.bitcast_convert_type(z[:, :d // 2].astype(BF16).astype(F32), U32)
    hi = lax.bitcast_convert_type(z[:, d // 2:].astype(BF16).astype(F32), U32)
    pk = lax.shift_right_logical(lo, jnp.uint32(16)) | (hi & jnp.uint32(0xFFFF0000))
    for s in range(PACK_ROWS):
        ref[pl.ds(s, tm, stride=PACK_ROWS), :] = pk[:, s * LANES:(s + 1) * LANES]


def _unpack_tokens(ref, first_row, tm):
    pk = jnp.concatenate([ref[pl.ds(first_row + s, tm, stride=PACK_ROWS), :] for s in range(PACK_ROWS)], axis=1)
    lo = lax.bitcast_convert_type(lax.shift_left(pk, jnp.uint32(16)), F32)
    hi = lax.bitcast_convert_type(pk & jnp.uint32(0xFFFF0000), F32)
    return lo, hi


def _proj_ln_kernel(*refs, n_in):
    a_refs, w_refs = refs[:n_in], refs[n_in:2 * n_in]
    x_ref, g_ref, b_ref, wr_ref, br_ref, o_ref, opk_ref, route_ref = refs[2 * n_in:]
    tm = x_ref.shape[0]
    sub = tm // PROJ_SPLIT
    for r in range(PROJ_SPLIT):
        rows = pl.ds(r * sub, sub)
        y = jnp.dot(a_refs[0][rows, :], w_refs[0][...], preferred_element_type=F32)
        for a_ref, w_ref in zip(a_refs[1:], w_refs[1:]):
            y = y + jnp.dot(a_ref[rows, :], w_ref[...], preferred_element_type=F32)
        z = DEEPNORM_ALPHA * x_ref[rows, :] + y
        xn = _layer_norm_rows(z, g_ref[...], b_ref[...])
        o_ref[rows, :] = xn
        _pack_tokens(xn, opk_ref.at[pl.ds(r * sub * PACK_ROWS, sub * PACK_ROWS)])
        route_ref[rows, :] = _route_rows(xn, wr_ref, br_ref)


def proj_residual_ln(acts, weights, x, g, b, w_route, b_route, *, tm):
    n, d = x.shape
    n_in = len(acts)
    const = lambda i: (0, 0)
    in_specs = [pl.BlockSpec((tm, a.shape[1]), lambda i: (i, 0)) for a in acts]
    in_specs += [pl.BlockSpec(w.shape, const, pipeline_mode=pl.Buffered(1)) for w in weights]
    in_specs += [pl.BlockSpec((tm, d), lambda i: (i, 0)), pl.BlockSpec((1, d), const), pl.BlockSpec((1, d), const),
                 pl.BlockSpec((d, LANES), const), pl.BlockSpec((1, LANES), const)]
    return pl.pallas_call(
        functools.partial(_proj_ln_kernel, n_in=n_in),
        grid=(n // tm,),
        in_specs=in_specs,
        out_specs=[pl.BlockSpec((tm, d), lambda i: (i, 0)), pl.BlockSpec((tm * PACK_ROWS, LANES), lambda i: (i, 0)),
                   pl.BlockSpec((tm, LANES), lambda i: (i, 0))],
        out_shape=[jax.ShapeDtypeStruct((n, d), F32), jax.ShapeDtypeStruct((n * PACK_ROWS, LANES), U32),
                   jax.ShapeDtypeStruct((n, LANES), F32)],
        compiler_params=_cparams(("parallel",)),
        name="proj_residual_ln",
    )(*acts, *weights, x, g.reshape(1, d), b.reshape(1, d), w_route, b_route)


def _rope_lanes(x, c_ref, s_up_ref, s_dn_ref):
    half = ROPE_DIM // 2
    return (x * c_ref[...] + pltpu.roll(x, half, axis=1) * s_up_ref[...]
            + pltpu.roll(x, LANES - half, axis=1) * s_dn_ref[...])


def _rms_rows(x, g):
    return x * lax.rsqrt(jnp.mean(x * x, axis=-1, keepdims=True) + RMS_EPS) * g


def _mla_in_kernel(x_ref, w_ref, gq_ref, gkv_ref, c_ref, su_ref, sd_ref, cq_ref, ckv_ref, kr_ref):
    h = jnp.dot(x_ref[...].astype(BF16), w_ref[...], preferred_element_type=F32)
    cq_ref[...] = _rms_rows(h[:, :Q_LORA], gq_ref[...]).astype(cq_ref.dtype)
    ckv_ref[...] = _rms_rows(h[:, Q_LORA:Q_LORA + KV_LORA], gkv_ref[...]).astype(ckv_ref.dtype)
    kr_ref[...] = _rope_lanes(h[:, Q_LORA + KV_LORA:], c_ref, su_ref, sd_ref).astype(kr_ref.dtype)


def mla_in_proj(x, w_in_pad, g_q, g_kv, rope_tabs, *, seq, tm):
    n, d = x.shape
    wn = w_in_pad.shape[1]
    npos = seq // tm
    tab = pl.BlockSpec((tm, LANES), lambda i: (i % npos, 0))
    return pl.pallas_call(
        _mla_in_kernel,
        grid=(n // tm,),
        in_specs=[pl.BlockSpec((tm, d), lambda i: (i, 0)), pl.BlockSpec((d, wn), lambda i: (0, 0)),
                  pl.BlockSpec((1, Q_LORA), lambda i: (0, 0)), pl.BlockSpec((1, KV_LORA), lambda i: (0, 0)),
                  tab, tab, tab],
        out_specs=[pl.BlockSpec((tm, Q_LORA), lambda i: (i, 0)), pl.BlockSpec((tm, KV_LORA), lambda i: (i, 0)),
                   pl.BlockSpec((tm, LANES), lambda i: (i, 0))],
        out_shape=[jax.ShapeDtypeStruct((n, Q_LORA), BF16), jax.ShapeDtypeStruct((n, KV_LORA), BF16),
                   jax.ShapeDtypeStruct((n, LANES), BF16)],
        compiler_params=_cparams(("parallel",)),
        name="mla_in_proj",
    )(x, w_in_pad, g_q.reshape(1, -1), g_kv.reshape(1, -1), *rope_tabs)


def _mla_q_kernel(cq_ref, w_ref, c_ref, su_ref, sd_ref, o_ref, *, heads):
    cq = cq_ref[...]
    for hh in range(heads):
        base = hh * 2 * LANES
        q = jnp.dot(cq, w_ref[:, base:base + 2 * LANES], preferred_element_type=F32)
        o_ref[:, base:base + LANES] = q[:, :LANES].astype(o_ref.dtype)
        o_ref[:, base + LANES:base + 2 * LANES] = _rope_lanes(q[:, LANES:], c_ref, su_ref, sd_ref).astype(o_ref.dtype)


def mla_q_proj(cq, w_uq_cat, rope_tabs, *, seq, tm, heads_per_step):
    n = cq.shape[0]
    wn = w_uq_cat.shape[1]
    tn = heads_per_step * 2 * LANES
    npos = seq // tm
    tab = pl.BlockSpec((tm, LANES), lambda i, j: (i % npos, 0))
    return pl.pallas_call(
        functools.partial(_mla_q_kernel, heads=heads_per_step),
        grid=(n // tm, wn // tn),
        in_specs=[pl.BlockSpec((tm, Q_LORA), lambda i, j: (i, 0)), pl.BlockSpec((Q_LORA, tn), lambda i, j: (0, j)),
                  tab, tab, tab],
        out_specs=pl.BlockSpec((tm, tn), lambda i, j: (i, j)),
        out_shape=jax.ShapeDtypeStruct((n, wn), BF16),
        compiler_params=_cparams(("parallel", "parallel")),
        name="mla_q_proj",
    )(cq, w_uq_cat, *rope_tabs)


def _mla_attn_kernel(q_ref, kn_ref, kr_ref, v_ref, o_ref, m_sc, acc_sc, *, tq, tk, heads):
    qi = pl.program_id(2)
    n_full = qi * (tq // tk)
    m_sc[...] = jnp.full_like(m_sc, NEG_INF)
    acc_sc[...] = jnp.zeros_like(acc_sc)
    ones = jnp.ones((tk, LANES), BF16)
    row_chunk = jnp.right_shift(lax.broadcasted_iota(I32, (tq, 1), 0), CHUNK_SHIFT)
    col_chunk = jnp.right_shift(lax.broadcasted_iota(I32, (1, tk), 1), CHUNK_SHIFT)

    def step(kj, diag):
        off = pl.multiple_of(kj * tk, tk)
        kr = kr_ref[pl.ds(off, tk), :]
        for hh in range(heads):
            k = jnp.concatenate([kn_ref[pl.ds(off, tk), hh * NOPE_DIM:(hh + 1) * NOPE_DIM], kr], axis=1)
            s = lax.dot_general(q_ref[:, hh * 2 * LANES:(hh + 1) * 2 * LANES], k, (((1,), (1,)), ((), ())),
                                preferred_element_type=F32)
            if diag is not None:
                s = jnp.where(col_chunk + diag * (tk // CHUNK) <= row_chunk, s, NEG_INF)
            m_prev = m_sc[hh]
            m_new = jnp.maximum(m_prev, jnp.max(s, axis=1, keepdims=True))
            alpha = jnp.exp2(m_prev - m_new)
            p = jnp.exp2(s - jnp.tile(m_new, (1, tk // LANES)))
            va = jnp.concatenate([v_ref[pl.ds(off, tk), hh * V_DIM:(hh + 1) * V_DIM], ones], axis=1)
            pv = jnp.dot(p.astype(BF16), va, preferred_element_type=F32)
            acc_sc[hh] = jnp.tile(alpha, (1, 2)) * acc_sc[hh] + pv
            m_sc[hh] = m_new

    def full_steps(i, c):
        for u in range(KV_UNROLL):
            step(KV_UNROLL * i + u, None)
        return c

    def full_step(kj, c):
        step(kj, None)
        return c

    lax.fori_loop(0, n_full // KV_UNROLL, full_steps, 0)
    lax.fori_loop(n_full - n_full % KV_UNROLL, n_full, full_step, 0)

    for dj in range(tq // tk):
        step(n_full + dj, dj)
    for hh in range(heads):
        acc = acc_sc[hh]
        o_ref[:, hh * V_DIM:(hh + 1) * V_DIM] = (acc[:, :V_DIM] / acc[:, V_DIM:]).astype(o_ref.dtype)


def mla_attention(q_cat, kv, k_rope, *, batch, seq, tq, tk, heads):
    n = q_cat.shape[0]
    nq = seq // tq
    n_hb = MLA_HEADS // heads
    return pl.pallas_call(
        functools.partial(_mla_attn_kernel, tq=tq, tk=tk, heads=heads),
        grid=(batch, n_hb, nq),
        in_specs=[
            pl.BlockSpec((tq, heads * 2 * LANES), lambda b, h, qi: (b * nq + qi, h)),
            pl.BlockSpec((seq, heads * NOPE_DIM), lambda b, h, qi: (b, h), pipeline_mode=pl.Buffered(1)),
            pl.BlockSpec((seq, LANES), lambda b, h, qi: (b, 0), pipeline_mode=pl.Buffered(1)),
            pl.BlockSpec((seq, heads * V_DIM), lambda b, h, qi: (b, n_hb + h), pipeline_mode=pl.Buffered(1)),
        ],
        out_specs=pl.BlockSpec((tq, heads * V_DIM), lambda b, h, qi: (b * nq + qi, h)),
        out_shape=jax.ShapeDtypeStruct((n, MLA_HEADS * V_DIM), BF16),
        scratch_shapes=[pltpu.VMEM((heads, tq, LANES), F32), pltpu.VMEM((heads, tq, 2 * LANES), F32)],
        compiler_params=_cparams(("parallel", "parallel", "arbitrary")),
        name="mla_attention",
    )(q_cat, kv, k_rope, kv)


def rope_tables(seq):
    half = ROPE_DIM // 2
    freq = ROPE_THETA ** (-jnp.arange(half, dtype=F32) / half)
    ang = jnp.arange(seq, dtype=F32)[:, None] * freq[None, :]
    cos, sin = jnp.cos(ang), jnp.sin(ang)
    z = jnp.zeros_like(cos)
    c = jnp.concatenate([cos, cos, z, z], axis=1)
    s_up = jnp.concatenate([z, sin, z, z], axis=1)
    s_dn = jnp.concatenate([-sin, z, z, z], axis=1)
    return c, s_up, s_dn


def _route_rows(xn, w_ref, b_ref):
    logits = jnp.dot(xn.astype(BF16), w_ref[...], preferred_element_type=F32) + b_ref[...]
    tm = logits.shape[0]
    lane = lax.broadcasted_iota(I32, (tm, LANES), 1)
    is_group = lane < N_GROUPS
    lg = jnp.where(is_group, logits, NEG_INF)
    mg = jnp.max(lg, axis=1, keepdims=True)
    g_sel = jnp.min(jnp.where(lg == mg, lane, LANES), axis=1, keepdims=True)
    p_group = 1.0 / jnp.sum(jnp.where(is_group, jnp.exp(lg - mg), 0.0), axis=1, keepdims=True)
    e_lane = lane - N_GROUPS
    in_sel = (e_lane >= 0) & (e_lane < N_EXPERTS) & (jnp.right_shift(e_lane, GROUP_SHIFT) == g_sel)
    le = jnp.where(in_sel, logits, NEG_INF)
    v1 = jnp.max(le, axis=1, keepdims=True)
    i1 = jnp.min(jnp.where(le == v1, lane, LANES), axis=1, keepdims=True)
    le2 = jnp.where(lane == i1, NEG_INF, le)
    v2 = jnp.max(le2, axis=1, keepdims=True)
    i2 = jnp.min(jnp.where(le2 == v2, lane, LANES), axis=1, keepdims=True)
    t = jnp.exp(v2 - v1)
    p1 = 1.0 / (1.0 + t)
    p2 = t / (1.0 + t)
    out = jnp.where(lane == 0, (i1 - N_GROUPS).astype(F32), 0.0)
    out = jnp.where(lane == 1, (i2 - N_GROUPS).astype(F32), out)
    out = jnp.where(lane == 2, p_group * p1, out)
    out = jnp.where(lane == 3, p_group * p2, out)
    return out


def _token_copy(src, src_tok, dst, dst_tok, sem):
    def first_row(tok):
        row = tok * PACK_ROWS
        return row if isinstance(row, int) else pl.multiple_of(row, PACK_ROWS)

    return pltpu.make_async_copy(src.at[pl.ds(first_row(src_tok), PACK_ROWS)],
                                 dst.at[pl.ds(first_row(dst_tok), PACK_ROWS)], sem)


def _dispatch_kernel(dest_ref, pad0_ref, padn_ref, nu_ref, x_ref, xs_ref, sem, sem_fill, *, tm, tb, max_blocks):
    base = pl.program_id(0) * tm * 2

    @pl.when(pl.program_id(0) == 0)
    def _():
        def fill(first_slot, n_slots):
            dst = xs_ref.at[pl.ds(pl.multiple_of(first_slot * PACK_ROWS, PACK_ROWS), n_slots * PACK_ROWS)]
            return pltpu.make_async_copy(x_ref.at[pl.ds(0, n_slots * PACK_ROWS)], dst, sem_fill)

        def pad_fills(ex):
            slot, left = pad0_ref[ex], padn_ref[ex]
            out = []
            for bit in reversed(range(tb.bit_length() - 1)):
                take = (left & (1 << bit)) != 0
                out.append((take, fill(slot, 1 << bit)))
                slot = slot + jnp.where(take, 1 << bit, 0)
            return out

        def for_pads(action):
            def body(ex, c):
                for take, cp in pad_fills(ex):
                    pl.when(take)(functools.partial(action, cp))
                return c
            lax.fori_loop(0, N_EXPERTS, body, 0)

        def for_tail(action):
            def body(g, c):
                action(fill(g * tb, tb))
                return c
            lax.fori_loop(nu_ref[0], max_blocks, body, 0)

        for_pads(lambda cp: cp.start())
        for_tail(lambda cp: cp.start())
        for_pads(lambda cp: cp.wait())
        for_tail(lambda cp: cp.wait())

    copies = [_token_copy(x_ref, a // 2, xs_ref, dest_ref[base + a], sem) for a in range(2 * tm)]
    for a, cp in enumerate(copies):
        cp.start(priority=a % 2)
    for cp in copies:
        cp.wait()


def moe_dispatch(x_pk, dest, pad_start, pad_len, n_used, max_blocks, *, tm, tb):
    n = x_pk.shape[0] // PACK_ROWS
    assert tm >= tb
    return pl.pallas_call(
        functools.partial(_dispatch_kernel, tm=tm, tb=tb, max_blocks=max_blocks),
        grid_spec=pltpu.PrefetchScalarGridSpec(
            num_scalar_prefetch=4,
            grid=(n // tm,),
            in_specs=[pl.BlockSpec((tm * PACK_ROWS, LANES), lambda i, *_: (i, 0))],
            out_specs=pl.BlockSpec(memory_space=pl.ANY),
            scratch_shapes=[pltpu.SemaphoreType.DMA(()), pltpu.SemaphoreType.DMA(())],
        ),
        out_shape=jax.ShapeDtypeStruct((max_blocks * tb * PACK_ROWS, LANES), U32),
        compiler_params=_cparams(("arbitrary",)),
        name="moe_dispatch",
    )(dest, pad_start, pad_len, n_used, x_pk)


def _expert_kernel(b0_ref, nb_ref, nu_ref, xs_ref, wg_hbm, wu_hbm, wd_hbm, ys_ref,
                   xbuf, ybuf, wg_buf, wu_buf, wd_buf, sem_in, sem_out, sem_w, *, tb, layer):
    e = pl.program_id(0)
    n_experts = pl.num_programs(0)
    b0, nb, n_used = b0_ref[e], nb_ref[e], nu_ref[0]
    rows = tb * PACK_ROWS

    def weight_copies(ex):
        slot = lax.rem(ex, WEIGHT_RING)
        copies = []
        for k, (hbm, buf) in enumerate(((wg_hbm, wg_buf), (wu_hbm, wu_buf), (wd_hbm, wd_buf))):
            half = buf.shape[1] // 2
            for part in range(2):
                rows_ = pl.ds(part * half, half)
                copies.append(pltpu.make_async_copy(hbm.at[layer, ex, rows_], buf.at[slot, rows_],
                                                    sem_w.at[2 * k + part, slot]))
        return copies

    def in_copy(g):
        src = xs_ref.at[pl.ds(pl.multiple_of(g * rows, rows), rows)]
        return pltpu.make_async_copy(src, xbuf.at[g % 2], sem_in.at[g % 2])

    def out_copy(g):
        dst = ys_ref.at[pl.ds(pl.multiple_of(g * rows, rows), rows)]
        return pltpu.make_async_copy(ybuf.at[g % 2], dst, sem_out.at[g % 2])

    @pl.when(e == 0)
    def _():
        in_copy(0).start()
        for ex in range(WEIGHT_RING - 1):
            for idx, cp in enumerate(weight_copies(ex)):
                cp.start(priority=idx % 2)

    @pl.when(e + WEIGHT_RING - 1 < n_experts)
    def _():
        for idx, cp in enumerate(weight_copies(e + WEIGHT_RING - 1)):
            cp.start(priority=idx % 2)

    for cp in weight_copies(e):
        cp.wait()
    slot = lax.rem(e, WEIGHT_RING)

    @pl.when(nb > 0)
    def _():
        def block(g, c):
            @pl.when(g + 1 < n_used)
            def _():
                in_copy(g + 1).start()

            in_copy(g).wait()

            @pl.when(g >= 2)
            def _():
                out_copy(g - 2).wait()

            lo, hi = _unpack_tokens(xbuf.at[g % 2], 0, tb)
            xb = jnp.concatenate([lo.astype(BF16), hi.astype(BF16)], axis=1)
            hg = jnp.dot(xb, wg_buf[slot].astype(BF16), preferred_element_type=F32)
            hu = jnp.dot(xb, wu_buf[slot].astype(BF16), preferred_element_type=F32)
            hb = (hg * jax.nn.sigmoid(hg) * hu).astype(BF16)
            _pack_tokens(jnp.dot(hb, wd_buf[slot].astype(BF16), preferred_element_type=F32), ybuf.at[g % 2])
            out_copy(g).start(priority=1)
            return c

        lax.fori_loop(b0, b0 + nb, block, 0)

    @pl.when(e == n_experts - 1)
    def _():
        @pl.when(n_used >= 2)
        def _():
            out_copy(n_used - 2).wait()

        out_copy(n_used - 1).wait()


def moe_experts(xs, first_block, n_blocks, n_used, w_gate, w_up, w_down, *, layer, tb):
    _, n_experts, d, hdim = w_gate.shape
    assert n_experts >= WEIGHT_RING
    ring = pltpu.VMEM((2, tb * PACK_ROWS, LANES), U32)
    any_space = pl.BlockSpec(memory_space=pl.ANY)
    return pl.pallas_call(
        functools.partial(_expert_kernel, tb=tb, layer=layer),
        grid_spec=pltpu.PrefetchScalarGridSpec(
            num_scalar_prefetch=3,
            grid=(n_experts,),
            in_specs=[any_space, any_space, any_space, any_space],
            out_specs=any_space,
            scratch_shapes=[ring, ring, pltpu.VMEM((WEIGHT_RING, d, hdim), F32), pltpu.VMEM((WEIGHT_RING, d, hdim), F32),
                            pltpu.VMEM((WEIGHT_RING, hdim, d), F32), pltpu.SemaphoreType.DMA((2,)),
                            pltpu.SemaphoreType.DMA((2,)), pltpu.SemaphoreType.DMA((6, WEIGHT_RING))],
        ),
        out_shape=jax.ShapeDtypeStruct(xs.shape, xs.dtype),
        input_output_aliases={3: 0},
        compiler_params=_cparams(("arbitrary",)),
        name="moe_experts",
    )(first_block, n_blocks, n_used, xs, w_gate, w_up, w_down)


def _combine_ln_kernel(dest_ref, x_ref, route_ref, g_ref, b_ref, ys_ref, o_ref, buf, sem, *, tm):
    i = pl.program_id(0)

    def gathers(step):
        slot = step % 2
        return [_token_copy(ys_ref, dest_ref[step * 2 * tm + a], buf.at[slot], (a % 2) * tm + a // 2, sem.at[slot])
                for a in range(2 * tm)]

    def start(step):
        for a, cp in enumerate(gathers(step)):
            cp.start(priority=a % 2)

    @pl.when(i == 0)
    def _():
        start(0)

    @pl.when(i + 1 < pl.num_programs(0))
    def _():
        start(i + 1)

    for cp in gathers(i):
        cp.wait()
    cur = buf.at[i % 2]
    g1 = route_ref[:, 2:3]
    g2 = route_ref[:, 3:4]
    lo1, hi1 = _unpack_tokens(cur, 0, tm)
    lo2, hi2 = _unpack_tokens(cur, tm * PACK_ROWS, tm)
    f = jnp.concatenate([lo1 * g1 + lo2 * g2, hi1 * g1 + hi2 * g2], axis=1)
    z = DEEPNORM_ALPHA * x_ref[...] + f
    o_ref[...] = _layer_norm_rows(z, g_ref[...], b_ref[...])


def moe_combine_ln(x, route, dest, ys, g, b, *, tm):
    n, d = x.shape
    return pl.pallas_call(
        functools.partial(_combine_ln_kernel, tm=tm),
        grid_spec=pltpu.PrefetchScalarGridSpec(
            num_scalar_prefetch=1,
            grid=(n // tm,),
            in_specs=[pl.BlockSpec((tm, d), lambda i, dest: (i, 0)), pl.BlockSpec((tm, LANES), lambda i, dest: (i, 0)),
                      pl.BlockSpec((1, d), lambda i, dest: (0, 0)), pl.BlockSpec((1, d), lambda i, dest: (0, 0)),
                      pl.BlockSpec(memory_space=pl.ANY)],
            out_specs=pl.BlockSpec((tm, d), lambda i, dest: (i, 0)),
            scratch_shapes=[pltpu.VMEM((2, 2 * tm * PACK_ROWS, LANES), U32), pltpu.SemaphoreType.DMA((2,))],
        ),
        out_shape=jax.ShapeDtypeStruct((n, d), F32),
        compiler_params=_cparams(("arbitrary",)),
        name="moe_combine_ln",
    )(dest, x, route, g.reshape(1, d), b.reshape(1, d), ys)


def _rank_kernel(route_ref, rank_ref, count_ref, carry_sc):
    @pl.when(pl.program_id(0) == 0)
    def _():
        carry_sc[...] = jnp.zeros_like(carry_sc)

    tm = route_ref.shape[0]
    lane = lax.broadcasted_iota(I32, (tm, LANES), 1)
    first = lane.astype(F32) == route_ref[:, 0:1]
    second = lane.astype(F32) == route_ref[:, 1:2]
    hits = (jnp.where(first, 1.0, 0.0) + jnp.where(second, 1.0, 0.0)).astype(BF16)
    row = lax.broadcasted_iota(I32, (tm, tm), 0)
    col = lax.broadcasted_iota(I32, (tm, tm), 1)
    earlier = jnp.where(col < row, 1.0, 0.0).astype(BF16)
    before = jnp.dot(earlier, hits, preferred_element_type=F32) + carry_sc[...]
    rank1 = jnp.sum(jnp.where(first, before, 0.0), axis=1, keepdims=True)
    rank2 = jnp.sum(jnp.where(second, before, 0.0), axis=1, keepdims=True)
    rank_ref[...] = jnp.where(lane == 0, rank1, jnp.where(lane == 1, rank2, 0.0))
    carry_sc[...] = carry_sc[...] + jnp.sum(hits.astype(F32), axis=0, keepdims=True)
    count_ref[...] = jnp.broadcast_to(carry_sc[...], count_ref.shape)


def moe_ranks(route, *, tm):
    n = route.shape[0]
    return pl.pallas_call(
        _rank_kernel,
        grid=(n // tm,),
        in_specs=[pl.BlockSpec((tm, LANES), lambda i: (i, 0))],
        out_specs=[pl.BlockSpec((tm, LANES), lambda i: (i, 0)), pl.BlockSpec((8, LANES), lambda i: (0, 0))],
        out_shape=[jax.ShapeDtypeStruct((n, LANES), F32), jax.ShapeDtypeStruct((8, LANES), F32)],
        scratch_shapes=[pltpu.VMEM((1, LANES), F32)],
        compiler_params=_cparams(("arbitrary",)),
        name="moe_ranks",
    )(route)


def routing_tables(route, *, tb):
    ranks, totals = moe_ranks(route, tm=512)
    experts = route[:, :2].astype(I32)
    counts = totals[0, :N_EXPERTS].astype(I32)
    padded = ((counts + tb - 1) // tb) * tb
    pends = jnp.cumsum(padded)
    pstarts = pends - padded
    dest = (jnp.take(pstarts, experts) + ranks[:, :2].astype(I32)).reshape(-1)
    n_used = (pends[-1] // tb).astype(I32).reshape(1)
    tables = dict(dest=dest, first_block=pstarts // tb, n_blocks=padded // tb, pad_start=pstarts + counts,
                  pad_len=padded - counts, n_used=n_used)
    return {k: v.astype(I32) for k, v in tables.items()}


def route_weights(w_group, b_group, w_expert, b_expert):
    d = w_group.shape[0]
    n_pad = LANES - N_GROUPS - N_EXPERTS
    w = jnp.concatenate([w_group, w_expert, jnp.zeros((d, n_pad), F32)], axis=1).astype(BF16)
    b = jnp.concatenate([b_group, b_expert, jnp.zeros((n_pad,), F32)]).reshape(1, LANES)
    return w, b


def hier_moe_ln(x, x_pk, route, w_gate, w_up, w_down, g, b, *, layer, tb=256, tm=256):
    t = routing_tables(route, tb=tb)
    max_blocks = (2 * x.shape[0]) // tb + N_EXPERTS
    xs = moe_dispatch(x_pk, t["dest"], t["pad_start"], t["pad_len"], t["n_used"], max_blocks, tm=2 * tm, tb=tb)
    ys = moe_experts(xs, t["first_block"], t["n_blocks"], t["n_used"], w_gate, w_up, w_down, layer=layer, tb=tb)
    return moe_combine_ln(x, route, t["dest"], ys, g, b, tm=tm)


def pool_band_layer(x, w_in, w_pool, pool_scale, rel_bias, w_o, g, b, w_route, b_route, *, batch, seq):
    qscale = jnp.concatenate([jnp.ones((POOL_WIDTH,), F32), jnp.full((ATT_WIDTH,), ATT_HEAD_DIM ** -0.5 * LOG2_E, F32),
                              jnp.ones((2 * ATT_WIDTH,), F32)])
    h = matmul(x, (w_in * qscale).astype(BF16), tm=512, tn=1024, out_dtype=BF16)
    a_out = pool_mixer(h, w_pool.astype(BF16), pool_scale.reshape(1, -1), seq=seq, tm=512)
    b_out = band_attention(h, band_bias_table(rel_bias), batch=batch, seq=seq, tq=512, pairs=8)
    w_o = w_o.astype(BF16)
    return proj_residual_ln([a_out, b_out], [w_o[:POOL_WIDTH], w_o[POOL_WIDTH:]], x, g, b, w_route, b_route, tm=512)


def mla_layer(x, w_in, g_q, g_kv, w_uq, w_uk, w_uv, w_o, g, b, w_route, b_route, *, batch, seq):
    d = x.shape[1]
    tabs = rope_tables(seq)
    w_in_pad = jnp.concatenate([w_in, jnp.zeros((d, LANES - ROPE_DIM), F32)], axis=1).astype(BF16)
    cq, ckv, k_rope = mla_in_proj(x, w_in_pad, g_q, g_kv, tabs, seq=seq, tm=512)
    w_uq_h = w_uq.reshape(Q_LORA, MLA_HEADS, NOPE_DIM + ROPE_DIM)
    w_uq_cat = jnp.concatenate([w_uq_h, jnp.zeros((Q_LORA, MLA_HEADS, 2 * LANES - NOPE_DIM - ROPE_DIM), F32)], axis=2)
    q_scale = (NOPE_DIM + ROPE_DIM) ** -0.5 * LOG2_E
    w_uq_cat = (w_uq_cat.reshape(Q_LORA, MLA_HEADS * 2 * LANES) * q_scale).astype(BF16)
    q_cat = mla_q_proj(cq, w_uq_cat, tabs, seq=seq, tm=512, heads_per_step=MLA_HEADS)
    w_kv = jnp.concatenate([w_uk, w_uv], axis=1).astype(BF16)
    kv = matmul(ckv, w_kv, tm=512, tn=1024, out_dtype=BF16)
    o = mla_attention(q_cat, kv, k_rope, batch=batch, seq=seq, tq=512, tk=512, heads=8)
    return proj_residual_ln([o], [w_o.astype(BF16)], x, g, b, w_route, b_route, tm=512)


def kernel(x, ab_w_in, ab_w_pool, ab_pool_scale, ab_rel_bias, ab_w_o, mla_w_in, mla_g_q, mla_g_kv, mla_w_uq, mla_w_uk, mla_w_uv, mla_w_o, ln1_g, ln1_b, moe_w_group, moe_b_group, moe_w_expert, moe_b_expert, moe_w_gate, moe_w_up, moe_w_down, ln2_g, ln2_b):
    batch, seq, d = x.shape
    x = x.reshape(batch * seq, d)
    for i in range(DEPTH):
        j = i // 2
        w_route, b_route = route_weights(moe_w_group[i], moe_b_group[i], moe_w_expert[i], moe_b_expert[i])
        if i % 2 == 0:
            x, x_pk, route = pool_band_layer(x, ab_w_in[j], ab_w_pool[j], ab_pool_scale[j], ab_rel_bias[j], ab_w_o[j],
                                             ln1_g[i], ln1_b[i], w_route, b_route, batch=batch, seq=seq)
        else:
            x, x_pk, route = mla_layer(x, mla_w_in[j], mla_g_q[j], mla_g_kv[j], mla_w_uq[j], mla_w_uk[j], mla_w_uv[j],
                                       mla_w_o[j], ln1_g[i], ln1_b[i], w_route, b_route, batch=batch, seq=seq)
        x = hier_moe_ln(x, x_pk, route, moe_w_gate, moe_w_up, moe_w_down, ln2_g[i], ln2_b[i], layer=i)
    return x.reshape(batch, seq, d)
```

```python
import functools

import jax
import jax.numpy as jnp
from jax import lax
from jax.experimental import pallas as pl
from jax.experimental.pallas import tpu as pltpu

F32 = jnp.float32
BF16 = jnp.bfloat16
I32 = jnp.int32
U32 = jnp.uint32

D_MODEL = 2048
DEPTH = 4
CHUNK = 64
CHUNK_SHIFT = 6
GROUP_SHIFT = 3
POOL_WINDOWS = (2, 4, 8, 16)
POOL_GROUP_DIM = 256
POOL_WIDTH = 1024
ATT_WIDTH = 1024
ATT_HEADS = 16
ATT_HEAD_DIM = 64
LEFT_CHUNKS = 8
REL_MAX = 128
MLA_HEADS = 16
Q_LORA = 512
KV_LORA = 512
NOPE_DIM = 128
ROPE_DIM = 64
V_DIM = 128
ROPE_THETA = 10000.0
N_GROUPS = 8
EXPERTS_PER_GROUP = 8
N_EXPERTS = 64
EXPERT_HIDDEN = 512
DEEPNORM_ALPHA = (2 * DEPTH) ** 0.25
LN_EPS = 1e-5
RMS_EPS = 1e-6

LANES = 128
PACK_ROWS = D_MODEL // 2 // LANES
PROJ_SPLIT = 2
KV_UNROLL = 2
WEIGHT_RING = 4
POOL_HALO = 16
BAND_SUB = 128
BAND_KEYS = BAND_SUB + LEFT_CHUNKS * CHUNK
NEG_INF = float("-inf")
LOG2_E = 1.4426950408889634
VMEM_LIMIT = 56 * 1024 * 1024


def _cparams(sem, vmem=VMEM_LIMIT):
    return pltpu.CompilerParams(dimension_semantics=sem, vmem_limit_bytes=vmem)


def _mm_kernel(x_ref, w_ref, o_ref, *, tn):
    xb = x_ref[...].astype(BF16)
    for c in range(o_ref.shape[1] // tn):
        cols = slice(c * tn, (c + 1) * tn)
        o_ref[:, cols] = jnp.dot(xb, w_ref[:, cols], preferred_element_type=F32).astype(o_ref.dtype)


def matmul(x, w, *, tm, tn, out_dtype):
    m, k = x.shape
    n = w.shape[1]
    return pl.pallas_call(
        functools.partial(_mm_kernel, tn=tn),
        grid=(m // tm,),
        in_specs=[pl.BlockSpec((tm, k), lambda i: (i, 0)),
                  pl.BlockSpec((k, n), lambda i: (0, 0), pipeline_mode=pl.Buffered(1))],
        out_specs=pl.BlockSpec((tm, n), lambda i: (i, 0)),
        out_shape=jax.ShapeDtypeStruct((m, n), out_dtype),
        compiler_params=_cparams(("parallel",)),
        name="matmul",
    )(x, w)


def _pool_kernel(u_ref, halo_ref, w_ref, scale_ref, o_ref, *, tm, seq):
    i = pl.program_id(0)
    pos0 = (i * tm) % seq
    u = u_ref[...].astype(F32)
    halo = jnp.where(pos0 == 0, 0.0, halo_ref[...].astype(F32))
    ue = jnp.concatenate([halo, u], axis=0)
    pos = pos0 + lax.broadcasted_iota(I32, (tm, 1), 0)
    for g, win in enumerate(POOL_WINDOWS):
        cols = slice(g * POOL_GROUP_DIM, (g + 1) * POOL_GROUP_DIM)
        s = ue[:, cols]
        shift = 1
        while shift < win:
            s = s + pltpu.roll(s, shift, axis=0)
            shift *= 2
        wsum = s[POOL_HALO:, :]
        count = jnp.minimum(pos + 1, win).astype(F32)
        d = wsum / count - u[:, cols]
        y = jnp.dot(d.astype(BF16), w_ref[g], preferred_element_type=F32) * scale_ref[:, cols]
        o_ref[:, cols] = y.astype(o_ref.dtype)


def pool_mixer(h, w_pool, pool_scale, *, seq, tm):
    n = h.shape[0]
    hb = tm // POOL_HALO
    return pl.pallas_call(
        functools.partial(_pool_kernel, tm=tm, seq=seq),
        grid=(n // tm,),
        in_specs=[
            pl.BlockSpec((tm, POOL_WIDTH), lambda i: (i, 0)),
            pl.BlockSpec((POOL_HALO, POOL_WIDTH), lambda i: (jnp.maximum(i * hb - 1, 0), 0)),
            pl.BlockSpec((len(POOL_WINDOWS), POOL_GROUP_DIM, POOL_GROUP_DIM), lambda i: (0, 0, 0)),
            pl.BlockSpec((1, POOL_WIDTH), lambda i: (0, 0)),
        ],
        out_specs=pl.BlockSpec((tm, POOL_WIDTH), lambda i: (i, 0)),
        out_shape=jax.ShapeDtypeStruct((n, POOL_WIDTH), BF16),
        compiler_params=_cparams(("parallel",)),
        name="pool_mixer",
    )(h, h, w_pool, pool_scale)


def _band_kernel(q_ref, kp_ref, kc_ref, vp_ref, vc_ref, bias_ref, o_ref, *, tq, pairs):
    i = pl.program_id(1)
    lane = lax.broadcasted_iota(I32, (1, LANES), 1)
    first_head = lane < ATT_HEAD_DIM
    col = lax.broadcasted_iota(I32, (1, BAND_KEYS), 1)
    left = LEFT_CHUNKS * CHUNK
    ones = jnp.ones((BAND_KEYS, LANES), BF16)
    for pr in range(pairs):
        lanes = slice(pr * LANES, (pr + 1) * LANES)
        kk = jnp.concatenate([kp_ref[:, lanes], kc_ref[:, lanes]], axis=0)
        vv = jnp.concatenate([vp_ref[:, lanes], vc_ref[:, lanes]], axis=0)
        for j in range(tq // BAND_SUB):
            rows = slice(j * BAND_SUB, (j + 1) * BAND_SUB)
            qs = q_ref[rows, lanes]
            off = tq - left + j * BAND_SUB
            kw = kk[off:off + BAND_KEYS, :]
            vw = jnp.concatenate([vv[off:off + BAND_KEYS, :], ones], axis=1)
            valid = (i * tq + j * BAND_SUB - left + col) >= 0
            outs = []
            for hh in range(2):
                qm = jnp.where(first_head if hh == 0 else jnp.logical_not(first_head), qs, jnp.zeros_like(qs))
                s = lax.dot_general(qm, kw, (((1,), (1,)), ((), ())), preferred_element_type=F32)
                s = jnp.where(valid, s + bias_ref[2 * pr + hh], NEG_INF)
                m = jnp.max(s, axis=1, keepdims=True)
                p = jnp.exp2(s - m)
                o = jnp.dot(p.astype(BF16), vw, preferred_element_type=F32)
                outs.append(o[:, :LANES] / o[:, LANES:])
            o_ref[rows, lanes] = jnp.where(first_head, outs[0], outs[1]).astype(o_ref.dtype)


def band_bias_table(rel_bias):
    heads, rel_size = rel_bias.shape
    r = jnp.arange(BAND_SUB, dtype=I32)[:, None]
    c = jnp.arange(BAND_KEYS, dtype=I32)[None, :]
    qc, kc = r // CHUNK, c // CHUNK
    allowed = (kc >= qc) & (kc <= qc + LEFT_CHUNKS)
    n_lo = BAND_SUB - CHUNK
    n_hi = BAND_KEYS - 1 - REL_MAX
    by_dist = jnp.concatenate([jnp.repeat(rel_bias[:, :1], n_lo, axis=1), rel_bias,
                               jnp.repeat(rel_bias[:, -1:], n_hi, axis=1)], axis=1).astype(F32)
    period = BAND_SUB + BAND_KEYS - 1
    w = jnp.concatenate([by_dist[:, :BAND_KEYS][:, ::-1], by_dist[:, BAND_KEYS:][:, ::-1]], axis=1)
    rows = jnp.tile(w, (1, BAND_SUB))[:, :BAND_SUB * (period - 1)].reshape(heads, BAND_SUB, period - 1)
    return jnp.where(allowed[None], rows[:, :, :BAND_KEYS] * LOG2_E, NEG_INF)


def band_attention(h, bias_tab, *, batch, seq, tq, pairs):
    n = h.shape[0]
    nq = seq // tq
    width = pairs * LANES
    qcol = POOL_WIDTH // width
    kcol = qcol + ATT_WIDTH // width
    vcol = kcol + ATT_WIDTH // width
    cur = lambda c0: (lambda b, i, p: (b * nq + i, c0 + p))
    prev = lambda c0: (lambda b, i, p: (b * nq + jnp.maximum(i - 1, 0), c0 + p))
    blk = (tq, width)
    return pl.pallas_call(
        functools.partial(_band_kernel, tq=tq, pairs=pairs),
        grid=(batch, nq, ATT_HEADS // (2 * pairs)),
        in_specs=[
            pl.BlockSpec(blk, cur(qcol)),
            pl.BlockSpec(blk, prev(kcol)), pl.BlockSpec(blk, cur(kcol)),
            pl.BlockSpec(blk, prev(vcol)), pl.BlockSpec(blk, cur(vcol)),
            pl.BlockSpec((2 * pairs, BAND_SUB, BAND_KEYS), lambda b, i, p: (p, 0, 0)),
        ],
        out_specs=pl.BlockSpec(blk, lambda b, i, p: (b * nq + i, p)),
        out_shape=jax.ShapeDtypeStruct((n, ATT_WIDTH), BF16),
        compiler_params=_cparams(("parallel", "parallel", "parallel")),
        name="band_attention",
    )(h, h, h, h, h, bias_tab)


def _layer_norm_rows(z, g, b):
    mu = jnp.mean(z, axis=-1, keepdims=True)
    zc = z - mu
    var = jnp.mean(zc * zc, axis=-1, keepdims=True)
    return zc * lax.rsqrt(var + LN_EPS) * g + b


def _pack_tokens(z, ref):
    tm, d = z.shape
    lo = lax.bitcast_convert_type(z[:, :d // 2].astype(BF16).astype(F32), U32)
    hi = lax.bitcast_convert_type(z[:, d // 2:].astype(BF16).astype(F32), U32)
    pk = lax.shift_right_logical(lo, jnp.uint32(16)) | (hi & jnp.uint32(0xFFFF0000))
    for s in range(PACK_ROWS):
        ref[pl.ds(s, tm, stride=PACK_ROWS), :] = pk[:, s * LANES:(s + 1) * LANES]


def _unpack_tokens(ref, first_row, tm):
    pk = jnp.concatenate([ref[pl.ds(first_row + s, tm, stride=PACK_ROWS), :] for s in range(PACK_ROWS)], axis=1)
    lo = lax.bitcast_convert_type(lax.shift_left(pk, jnp.uint32(16)), F32)
    hi = lax.bitcast_convert_type(pk & jnp.uint32(0xFFFF0000), F32)
    return lo, hi


def _proj_ln_kernel(*refs, n_in):
    a_refs, w_refs = refs[:n_in], refs[n_in:2 * n_in]
    x_ref, g_ref, b_ref, wr_ref, br_ref, o_ref, opk_ref, route_ref = refs[2 * n_in:]
    tm = x_ref.shape[0]
    sub = tm // PROJ_SPLIT
    for r in range(PROJ_SPLIT):
        rows = pl.ds(r * sub, sub)
        y = jnp.dot(a_refs[0][rows, :], w_refs[0][...], preferred_element_type=F32)
        for a_ref, w_ref in zip(a_refs[1:], w_refs[1:]):
            y = y + jnp.dot(a_ref[rows, :], w_ref[...], preferred_element_type=F32)
        z = DEEPNORM_ALPHA * x_ref[rows, :] + y
        xn = _layer_norm_rows(z, g_ref[...], b_ref[...])
        o_ref[rows, :] = xn
        _pack_tokens(xn, opk_ref.at[pl.ds(r * sub * PACK_ROWS, sub * PACK_ROWS)])
        route_ref[rows, :] = _route_rows(xn, wr_ref, br_ref)


def proj_residual_ln(acts, weights, x, g, b, w_route, b_route, *, tm):
    n, d = x.shape
    n_in = len(acts)
    const = lambda i: (0, 0)
    in_specs = [pl.BlockSpec((tm, a.shape[1]), lambda i: (i, 0)) for a in acts]
    in_specs += [pl.BlockSpec(w.shape, const, pipeline_mode=pl.Buffered(1)) for w in weights]
    in_specs += [pl.BlockSpec((tm, d), lambda i: (i, 0)), pl.BlockSpec((1, d), const), pl.BlockSpec((1, d), const),
                 pl.BlockSpec((d, LANES), const), pl.BlockSpec((1, LANES), const)]
    return pl.pallas_call(
        functools.partial(_proj_ln_kernel, n_in=n_in),
        grid=(n // tm,),
        in_specs=in_specs,
        out_specs=[pl.BlockSpec((tm, d), lambda i: (i, 0)), pl.BlockSpec((tm * PACK_ROWS, LANES), lambda i: (i, 0)),
                   pl.BlockSpec((tm, LANES), lambda i: (i, 0))],
        out_shape=[jax.ShapeDtypeStruct((n, d), F32), jax.ShapeDtypeStruct((n * PACK_ROWS, LANES), U32),
                   jax.ShapeDtypeStruct((n, LANES), F32)],
        compiler_params=_cparams(("parallel",)),
        name="proj_residual_ln",
    )(*acts, *weights, x, g.reshape(1, d), b.reshape(1, d), w_route, b_route)


def _rope_lanes(x, c_ref, s_up_ref, s_dn_ref):
    half = ROPE_DIM // 2
    return (x * c_ref[...] + pltpu.roll(x, half, axis=1) * s_up_ref[...]
            + pltpu.roll(x, LANES - half, axis=1) * s_dn_ref[...])


def _rms_rows(x, g):
    return x * lax.rsqrt(jnp.mean(x * x, axis=-1, keepdims=True) + RMS_EPS) * g


def _mla_in_kernel(x_ref, w_ref, gq_ref, gkv_ref, c_ref, su_ref, sd_ref, cq_ref, ckv_ref, kr_ref):
    h = jnp.dot(x_ref[...].astype(BF16), w_ref[...], preferred_element_type=F32)
    cq_ref[...] = _rms_rows(h[:, :Q_LORA], gq_ref[...]).astype(cq_ref.dtype)
    ckv_ref[...] = _rms_rows(h[:, Q_LORA:Q_LORA + KV_LORA], gkv_ref[...]).astype(ckv_ref.dtype)
    kr_ref[...] = _rope_lanes(h[:, Q_LORA + KV_LORA:], c_ref, su_ref, sd_ref).astype(kr_ref.dtype)


def mla_in_proj(x, w_in_pad, g_q, g_kv, rope_tabs, *, seq, tm):
    n, d = x.shape
    wn = w_in_pad.shape[1]
    npos = seq // tm
    tab = pl.BlockSpec((tm, LANES), lambda i: (i % npos, 0))
    return pl.pallas_call(
        _mla_in_kernel,
        grid=(n // tm,),
        in_specs=[pl.BlockSpec((tm, d), lambda i: (i, 0)), pl.BlockSpec((d, wn), lambda i: (0, 0)),
                  pl.BlockSpec((1, Q_LORA), lambda i: (0, 0)), pl.BlockSpec((1, KV_LORA), lambda i: (0, 0)),
                  tab, tab, tab],
        out_specs=[pl.BlockSpec((tm, Q_LORA), lambda i: (i, 0)), pl.BlockSpec((tm, KV_LORA), lambda i: (i, 0)),
                   pl.BlockSpec((tm, LANES), lambda i: (i, 0))],
        out_shape=[jax.ShapeDtypeStruct((n, Q_LORA), BF16), jax.ShapeDtypeStruct((n, KV_LORA), BF16),
                   jax.ShapeDtypeStruct((n, LANES), BF16)],
        compiler_params=_cparams(("parallel",)),
        name="mla_in_proj",
    )(x, w_in_pad, g_q.reshape(1, -1), g_kv.reshape(1, -1), *rope_tabs)


def _mla_q_kernel(cq_ref, w_ref, c_ref, su_ref, sd_ref, o_ref, *, heads):
    cq = cq_ref[...]
    for hh in range(heads):
        base = hh * 2 * LANES
        q = jnp.dot(cq, w_ref[:, base:base + 2 * LANES], preferred_element_type=F32)
        o_ref[:, base:base + LANES] = q[:, :LANES].astype(o_ref.dtype)
        o_ref[:, base + LANES:base + 2 * LANES] = _rope_lanes(q[:, LANES:], c_ref, su_ref, sd_ref).astype(o_ref.dtype)


def mla_q_proj(cq, w_uq_cat, rope_tabs, *, seq, tm, heads_per_step):
    n = cq.shape[0]
    wn = w_uq_cat.shape[1]
    tn = heads_per_step * 2 * LANES
    npos = seq // tm
    tab = pl.BlockSpec((tm, LANES), lambda i, j: (i % npos, 0))
    return pl.pallas_call(
        functools.partial(_mla_q_kernel, heads=heads_per_step),
        grid=(n // tm, wn // tn),
        in_specs=[pl.BlockSpec((tm, Q_LORA), lambda i, j: (i, 0)), pl.BlockSpec((Q_LORA, tn), lambda i, j: (0, j)),
                  tab, tab, tab],
        out_specs=pl.BlockSpec((tm, tn), lambda i, j: (i, j)),
        out_shape=jax.ShapeDtypeStruct((n, wn), BF16),
        compiler_params=_cparams(("parallel", "parallel")),
        name="mla_q_proj",
    )(cq, w_uq_cat, *rope_tabs)


def _mla_attn_kernel(q_ref, kn_ref, kr_ref, v_ref, o_ref, m_sc, acc_sc, *, tq, tk, heads):
    qi = pl.program_id(2)
    n_full = qi * (tq // tk)
    m_sc[...] = jnp.full_like(m_sc, NEG_INF)
    acc_sc[...] = jnp.zeros_like(acc_sc)
    ones = jnp.ones((tk, LANES), BF16)
    row_chunk = jnp.right_shift(lax.broadcasted_iota(I32, (tq, 1), 0), CHUNK_SHIFT)
    col_chunk = jnp.right_shift(lax.broadcasted_iota(I32, (1, tk), 1), CHUNK_SHIFT)

    def step(kj, diag):
        off = pl.multiple_of(kj * tk, tk)
        kr = kr_ref[pl.ds(off, tk), :]
        for hh in range(heads):
            k = jnp.concatenate([kn_ref[pl.ds(off, tk), hh * NOPE_DIM:(hh + 1) * NOPE_DIM], kr], axis=1)
            s = lax.dot_general(q_ref[:, hh * 2 * LANES:(hh + 1) * 2 * LANES], k, (((1,), (1,)), ((), ())),
                                preferred_element_type=F32)
            if diag is not None:
                s = jnp.where(col_chunk + diag * (tk // CHUNK) <= row_chunk, s, NEG_INF)
            m_prev = m_sc[hh]
            m_new = jnp.maximum(m_prev, jnp.max(s, axis=1, keepdims=True))
            alpha = jnp.exp2(m_prev - m_new)
            p = jnp.exp2(s - jnp.tile(m_new, (1, tk // LANES)))
            va = jnp.concatenate([v_ref[pl.ds(off, tk), hh * V_DIM:(hh + 1) * V_DIM], ones], axis=1)
            pv = jnp.dot(p.astype(BF16), va, preferred_element_type=F32)
            acc_sc[hh] = jnp.tile(alpha, (1, 2)) * acc_sc[hh] + pv
            m_sc[hh] = m_new

    def full_steps(i, c):
        for u in range(KV_UNROLL):
            step(KV_UNROLL * i + u, None)
        return c

    def full_step(kj, c):
        step(kj, None)
        return c

    lax.fori_loop(0, n_full // KV_UNROLL, full_steps, 0)
    lax.fori_loop(n_full - n_full % KV_UNROLL, n_full, full_step, 0)

    for dj in range(tq // tk):
        step(n_full + dj, dj)
    for hh in range(heads):
        acc = acc_sc[hh]
        o_ref[:, hh * V_DIM:(hh + 1) * V_DIM] = (acc[:, :V_DIM] / acc[:, V_DIM:]).astype(o_ref.dtype)


def mla_attention(q_cat, kv, k_rope, *, batch, seq, tq, tk, heads):
    n = q_cat.shape[0]
    nq = seq // tq
    n_hb = MLA_HEADS // heads
    return pl.pallas_call(
        functools.partial(_mla_attn_kernel, tq=tq, tk=tk, heads=heads),
        grid=(batch, n_hb, nq),
        in_specs=[
            pl.BlockSpec((tq, heads * 2 * LANES), lambda b, h, qi: (b * nq + qi, h)),
            pl.BlockSpec((seq, heads * NOPE_DIM), lambda b, h, qi: (b, h), pipeline_mode=pl.Buffered(1)),
            pl.BlockSpec((seq, LANES), lambda b, h, qi: (b, 0), pipeline_mode=pl.Buffered(1)),
            pl.BlockSpec((seq, heads * V_DIM), lambda b, h, qi: (b, n_hb + h), pipeline_mode=pl.Buffered(1)),
        ],
        out_specs=pl.BlockSpec((tq, heads * V_DIM), lambda b, h, qi: (b * nq + qi, h)),
        out_shape=jax.ShapeDtypeStruct((n, MLA_HEADS * V_DIM), BF16),
        scratch_shapes=[pltpu.VMEM((heads, tq, LANES), F32), pltpu.VMEM((heads, tq, 2 * LANES), F32)],
        compiler_params=_cparams(("parallel", "parallel", "arbitrary")),
        name="mla_attention",
    )(q_cat, kv, k_rope, kv)


def rope_tables(seq):
    half = ROPE_DIM // 2
    freq = ROPE_THETA ** (-jnp.arange(half, dtype=F32) / half)
    ang = jnp.arange(seq, dtype=F32)[:, None] * freq[None, :]
    cos, sin = jnp.cos(ang), jnp.sin(ang)
    z = jnp.zeros_like(cos)
    c = jnp.concatenate([cos, cos, z, z], axis=1)
    s_up = jnp.concatenate([z, sin, z, z], axis=1)
    s_dn = jnp.concatenate([-sin, z, z, z], axis=1)
    return c, s_up, s_dn


def _route_rows(xn, w_ref, b_ref):
    logits = jnp.dot(xn.astype(BF16), w_ref[...], preferred_element_type=F32) + b_ref[...]
    tm = logits.shape[0]
    lane = lax.broadcasted_iota(I32, (tm, LANES), 1)
    is_group = lane < N_GROUPS
    lg = jnp.where(is_group, logits, NEG_INF)
    mg = jnp.max(lg, axis=1, keepdims=True)
    g_sel = jnp.min(jnp.where(lg == mg, lane, LANES), axis=1, keepdims=True)
    p_group = 1.0 / jnp.sum(jnp.where(is_group, jnp.exp(lg - mg), 0.0), axis=1, keepdims=True)
    e_lane = lane - N_GROUPS
    in_sel = (e_lane >= 0) & (e_lane < N_EXPERTS) & (jnp.right_shift(e_lane, GROUP_SHIFT) == g_sel)
    le = jnp.where(in_sel, logits, NEG_INF)
    v1 = jnp.max(le, axis=1, keepdims=True)
    i1 = jnp.min(jnp.where(le == v1, lane, LANES), axis=1, keepdims=True)
    le2 = jnp.where(lane == i1, NEG_INF, le)
    v2 = jnp.max(le2, axis=1, keepdims=True)
    i2 = jnp.min(jnp.where(le2 == v2, lane, LANES), axis=1, keepdims=True)
    t = jnp.exp(v2 - v1)
    p1 = 1.0 / (1.0 + t)
    p2 = t / (1.0 + t)
    out = jnp.where(lane == 0, (i1 - N_GROUPS).astype(F32), 0.0)
    out = jnp.where(lane == 1, (i2 - N_GROUPS).astype(F32), out)
    out = jnp.where(lane == 2, p_group * p1, out)
    out = jnp.where(lane == 3, p_group * p2, out)
    return out


def _token_copy(src, src_tok, dst, dst_tok, sem):
    def first_row(tok):
        row = tok * PACK_ROWS
        return row if isinstance(row, int) else pl.multiple_of(row, PACK_ROWS)

    return pltpu.make_async_copy(src.at[pl.ds(first_row(src_tok), PACK_ROWS)],
                                 dst.at[pl.ds(first_row(dst_tok), PACK_ROWS)], sem)


def _dispatch_kernel(dest_ref, pad0_ref, padn_ref, nu_ref, x_ref, xs_ref, sem, sem_fill, *, tm, tb, max_blocks):
    base = pl.program_id(0) * tm * 2

    @pl.when(pl.program_id(0) == 0)
    def _():
        def fill(first_slot, n_slots):
            dst = xs_ref.at[pl.ds(pl.multiple_of(first_slot * PACK_ROWS, PACK_ROWS), n_slots * PACK_ROWS)]
            return pltpu.make_async_copy(x_ref.at[pl.ds(0, n_slots * PACK_ROWS)], dst, sem_fill)

        def pad_fills(ex):
            slot, left = pad0_ref[ex], padn_ref[ex]
            out = []
            for bit in reversed(range(tb.bit_length() - 1)):
                take = (left & (1 << bit)) != 0
                out.append((take, fill(slot, 1 << bit)))
                slot = slot + jnp.where(take, 1 << bit, 0)
            return out

        def for_pads(action):
            def body(ex, c):
                for take, cp in pad_fills(ex):
                    pl.when(take)(functools.partial(action, cp))
                return c
            lax.fori_loop(0, N_EXPERTS, body, 0)

        def for_tail(action):
            def body(g, c):
                action(fill(g * tb, tb))
                return c
            lax.fori_loop(nu_ref[0], max_blocks, body, 0)

        for_pads(lambda cp: cp.start())
        for_tail(lambda cp: cp.start())
        for_pads(lambda cp: cp.wait())
        for_tail(lambda cp: cp.wait())

    copies = [_token_copy(x_ref, a // 2, xs_ref, dest_ref[base + a], sem) for a in range(2 * tm)]
    for a, cp in enumerate(copies):
        cp.start(priority=a % 2)
    for cp in copies:
        cp.wait()


def moe_dispatch(x_pk, dest, pad_start, pad_len, n_used, max_blocks, *, tm, tb):
    n = x_pk.shape[0] // PACK_ROWS
    assert tm >= tb
    return pl.pallas_call(
        functools.partial(_dispatch_kernel, tm=tm, tb=tb, max_blocks=max_blocks),
        grid_spec=pltpu.PrefetchScalarGridSpec(
            num_scalar_prefetch=4,
            grid=(n // tm,),
            in_specs=[pl.BlockSpec((tm * PACK_ROWS, LANES), lambda i, *_: (i, 0))],
            out_specs=pl.BlockSpec(memory_space=pl.ANY),
            scratch_shapes=[pltpu.SemaphoreType.DMA(()), pltpu.SemaphoreType.DMA(())],
        ),
        out_shape=jax.ShapeDtypeStruct((max_blocks * tb * PACK_ROWS, LANES), U32),
        compiler_params=_cparams(("arbitrary",)),
        name="moe_dispatch",
    )(dest, pad_start, pad_len, n_used, x_pk)


def _expert_kernel(b0_ref, nb_ref, nu_ref, xs_ref, wg_hbm, wu_hbm, wd_hbm, ys_ref,
                   xbuf, ybuf, wg_buf, wu_buf, wd_buf, sem_in, sem_out, sem_w, *, tb, layer):
    e = pl.program_id(0)
    n_experts = pl.num_programs(0)
    b0, nb, n_used = b0_ref[e], nb_ref[e], nu_ref[0]
    rows = tb * PACK_ROWS

    def weight_copies(ex):
        slot = lax.rem(ex, WEIGHT_RING)
        copies = []
        for k, (hbm, buf) in enumerate(((wg_hbm, wg_buf), (wu_hbm, wu_buf), (wd_hbm, wd_buf))):
            half = buf.shape[1] // 2
            for part in range(2):
                rows_ = pl.ds(part * half, half)
                copies.append(pltpu.make_async_copy(hbm.at[layer, ex, rows_], buf.at[slot, rows_],
                                                    sem_w.at[2 * k + part, slot]))
        return copies

    def in_copy(g):
        src = xs_ref.at[pl.ds(pl.multiple_of(g * rows, rows), rows)]
        return pltpu.make_async_copy(src, xbuf.at[g % 2], sem_in.at[g % 2])

    def out_copy(g):
        dst = ys_ref.at[pl.ds(pl.multiple_of(g * rows, rows), rows)]
        return pltpu.make_async_copy(ybuf.at[g % 2], dst, sem_out.at[g % 2])

    @pl.when(e == 0)
    def _():
        in_copy(0).start()
        for ex in range(WEIGHT_RING - 1):
            for idx, cp in enumerate(weight_copies(ex)):
                cp.start(priority=idx % 2)

    @pl.when(e + WEIGHT_RING - 1 < n_experts)
    def _():
        for idx, cp in enumerate(weight_copies(e + WEIGHT_RING - 1)):
            cp.start(priority=idx % 2)

    for cp in weight_copies(e):
        cp.wait()
    slot = lax.rem(e, WEIGHT_RING)

    @pl.when(nb > 0)
    def _():
        def block(g, c):
            @pl.when(g + 1 < n_used)
            def _():
                in_copy(g + 1).start()

            in_copy(g).wait()

            @pl.when(g >= 2)
            def _():
                out_copy(g - 2).wait()

            lo, hi = _unpack_tokens(xbuf.at[g % 2], 0, tb)
            xb = jnp.concatenate([lo.astype(BF16), hi.astype(BF16)], axis=1)
            hg = jnp.dot(xb, wg_buf[slot].astype(BF16), preferred_element_type=F32)
            hu = jnp.dot(xb, wu_buf[slot].astype(BF16), preferred_element_type=F32)
            hb = (hg * jax.nn.sigmoid(hg) * hu).astype(BF16)
            _pack_tokens(jnp.dot(hb, wd_buf[slot].astype(BF16), preferred_element_type=F32), ybuf.at[g % 2])
            out_copy(g).start(priority=1)
            return c

        lax.fori_loop(b0, b0 + nb, block, 0)

    @pl.when(e == n_experts - 1)
    def _():
        @pl.when(n_used >= 2)
        def _():
            out_copy(n_used - 2).wait()

        out_copy(n_used - 1).wait()


def moe_experts(xs, first_block, n_blocks, n_used, w_gate, w_up, w_down, *, layer, tb):
    _, n_experts, d, hdim = w_gate.shape
    assert n_experts >= WEIGHT_RING
    ring = pltpu.VMEM((2, tb * PACK_ROWS, LANES), U32)
    any_space = pl.BlockSpec(memory_space=pl.ANY)
    return pl.pallas_call(
        functools.partial(_expert_kernel, tb=tb, layer=layer),
        grid_spec=pltpu.PrefetchScalarGridSpec(
            num_scalar_prefetch=3,
            grid=(n_experts,),
            in_specs=[any_space, any_space, any_space, any_space],
            out_specs=any_space,
            scratch_shapes=[ring, ring, pltpu.VMEM((WEIGHT_RING, d, hdim), F32), pltpu.VMEM((WEIGHT_RING, d, hdim), F32),
                            pltpu.VMEM((WEIGHT_RING, hdim, d), F32), pltpu.SemaphoreType.DMA((2,)),
                            pltpu.SemaphoreType.DMA((2,)), pltpu.SemaphoreType.DMA((6, WEIGHT_RING))],
        ),
        out_shape=jax.ShapeDtypeStruct(xs.shape, xs.dtype),
        input_output_aliases={3: 0},
        compiler_params=_cparams(("arbitrary",)),
        name="moe_experts",
    )(first_block, n_blocks, n_used, xs, w_gate, w_up, w_down)


def _combine_ln_kernel(dest_ref, x_ref, route_ref, g_ref, b_ref, ys_ref, o_ref, buf, sem, *, tm):
    i = pl.program_id(0)

    def gathers(step):
        slot = step % 2
        return [_token_copy(ys_ref, dest_ref[step * 2 * tm + a], buf.at[slot], (a % 2) * tm + a // 2, sem.at[slot])
                for a in range(2 * tm)]

    def start(step):
        for a, cp in enumerate(gathers(step)):
            cp.start(priority=a % 2)

    @pl.when(i == 0)
    def _():
        start(0)

    @pl.when(i + 1 < pl.num_programs(0))
    def _():
        start(i + 1)

    for cp in gathers(i):
        cp.wait()
    cur = buf.at[i % 2]
    g1 = route_ref[:, 2:3]
    g2 = route_ref[:, 3:4]
    lo1, hi1 = _unpack_tokens(cur, 0, tm)
    lo2, hi2 = _unpack_tokens(cur, tm * PACK_ROWS, tm)
    f = jnp.concatenate([lo1 * g1 + lo2 * g2, hi1 * g1 + hi2 * g2], axis=1)
    z = DEEPNORM_ALPHA * x_ref[...] + f
    o_ref[...] = _layer_norm_rows(z, g_ref[...], b_ref[...])


def moe_combine_ln(x, route, dest, ys, g, b, *, tm):
    n, d = x.shape
    return pl.pallas_call(
        functools.partial(_combine_ln_kernel, tm=tm),
        grid_spec=pltpu.PrefetchScalarGridSpec(
            num_scalar_prefetch=1,
            grid=(n // tm,),
            in_specs=[pl.BlockSpec((tm, d), lambda i, dest: (i, 0)), pl.BlockSpec((tm, LANES), lambda i, dest: (i, 0)),
                      pl.BlockSpec((1, d), lambda i, dest: (0, 0)), pl.BlockSpec((1, d), lambda i, dest: (0, 0)),
                      pl.BlockSpec(memory_space=pl.ANY)],
            out_specs=pl.BlockSpec((tm, d), lambda i, dest: (i, 0)),
            scratch_shapes=[pltpu.VMEM((2, 2 * tm * PACK_ROWS, LANES), U32), pltpu.SemaphoreType.DMA((2,))],
        ),
        out_shape=jax.ShapeDtypeStruct((n, d), F32),
        compiler_params=_cparams(("arbitrary",)),
        name="moe_combine_ln",
    )(dest, x, route, g.reshape(1, d), b.reshape(1, d), ys)


def _rank_kernel(route_ref, rank_ref, count_ref, carry_sc):
    @pl.when(pl.program_id(0) == 0)
    def _():
        carry_sc[...] = jnp.zeros_like(carry_sc)

    tm = route_ref.shape[0]
    lane = lax.broadcasted_iota(I32, (tm, LANES), 1)
    first = lane.astype(F32) == route_ref[:, 0:1]
    second = lane.astype(F32) == route_ref[:, 1:2]
    hits = (jnp.where(first, 1.0, 0.0) + jnp.where(second, 1.0, 0.0)).astype(BF16)
    row = lax.broadcasted_iota(I32, (tm, tm), 0)
    col = lax.broadcasted_iota(I32, (tm, tm), 1)
    earlier = jnp.where(col < row, 1.0, 0.0).astype(BF16)
    before = jnp.dot(earlier, hits, preferred_element_type=F32) + carry_sc[...]
    rank1 = jnp.sum(jnp.where(first, before, 0.0), axis=1, keepdims=True)
    rank2 = jnp.sum(jnp.where(second, before, 0.0), axis=1, keepdims=True)
    rank_ref[...] = jnp.where(lane == 0, rank1, jnp.where(lane == 1, rank2, 0.0))
    carry_sc[...] = carry_sc[...] + jnp.sum(hits.astype(F32), axis=0, keepdims=True)
    count_ref[...] = jnp.broadcast_to(carry_sc[...], count_ref.shape)


def moe_ranks(route, *, tm):
    n = route.shape[0]
    return pl.pallas_call(
        _rank_kernel,
        grid=(n // tm,),
        in_specs=[pl.BlockSpec((tm, LANES), lambda i: (i, 0))],
        out_specs=[pl.BlockSpec((tm, LANES), lambda i: (i, 0)), pl.BlockSpec((8, LANES), lambda i: (0, 0))],
        out_shape=[jax.ShapeDtypeStruct((n, LANES), F32), jax.ShapeDtypeStruct((8, LANES), F32)],
        scratch_shapes=[pltpu.VMEM((1, LANES), F32)],
        compiler_params=_cparams(("arbitrary",)),
        name="moe_ranks",
    )(route)


def routing_tables(route, *, tb):
    ranks, totals = moe_ranks(route, tm=512)
    experts = route[:, :2].astype(I32)
    counts = totals[0, :N_EXPERTS].astype(I32)
    padded = ((counts + tb - 1) // tb) * tb
    pends = jnp.cumsum(padded)
    pstarts = pends - padded
    is_expert = experts[:, :, None] == jnp.arange(N_EXPERTS, dtype=I32)[None, None, :]
    first_slot = jnp.sum(jnp.where(is_expert, pstarts[None, None, :], 0), axis=2)
    dest = (first_slot + ranks[:, :2].astype(I32)).reshape(-1)
    n_used = (pends[-1] // tb).astype(I32).reshape(1)
    tables = dict(dest=dest, first_block=pstarts // tb, n_blocks=padded // tb, pad_start=pstarts + counts,
                  pad_len=padded - counts, n_used=n_used)
    return {k: v.astype(I32) for k, v in tables.items()}


def route_weights(w_group, b_group, w_expert, b_expert):
    d = w_group.shape[0]
    n_pad = LANES - N_GROUPS - N_EXPERTS
    w = jnp.concatenate([w_group, w_expert, jnp.zeros((d, n_pad), F32)], axis=1).astype(BF16)
    b = jnp.concatenate([b_group, b_expert, jnp.zeros((n_pad,), F32)]).reshape(1, LANES)
    return w, b


def hier_moe_ln(x, x_pk, route, w_gate, w_up, w_down, g, b, *, layer, tb=256, tm=256):
    t = routing_tables(route, tb=tb)
    max_blocks = (2 * x.shape[0]) // tb + N_EXPERTS
    xs = moe_dispatch(x_pk, t["dest"], t["pad_start"], t["pad_len"], t["n_used"], max_blocks, tm=2 * tm, tb=tb)
    ys = moe_experts(xs, t["first_block"], t["n_blocks"], t["n_used"], w_gate, w_up, w_down, layer=layer, tb=tb)
    return moe_combine_ln(x, route, t["dest"], ys, g, b, tm=tm)


def pool_band_layer(x, w_in, w_pool, pool_scale, rel_bias, w_o, g, b, w_route, b_route, *, batch, seq):
    qscale = jnp.concatenate([jnp.ones((POOL_WIDTH,), F32), jnp.full((ATT_WIDTH,), ATT_HEAD_DIM ** -0.5 * LOG2_E, F32),
                              jnp.ones((2 * ATT_WIDTH,), F32)])
    h = matmul(x, (w_in * qscale).astype(BF16), tm=512, tn=1024, out_dtype=BF16)
    a_out = pool_mixer(h, w_pool.astype(BF16), pool_scale.reshape(1, -1), seq=seq, tm=512)
    b_out = band_attention(h, band_bias_table(rel_bias), batch=batch, seq=seq, tq=512, pairs=8)
    w_o = w_o.astype(BF16)
    return proj_residual_ln([a_out, b_out], [w_o[:POOL_WIDTH], w_o[POOL_WIDTH:]], x, g, b, w_route, b_route, tm=512)


def mla_layer(x, w_in, g_q, g_kv, w_uq, w_uk, w_uv, w_o, g, b, w_route, b_route, *, batch, seq):
    d = x.shape[1]
    tabs = rope_tables(seq)
    w_in_pad = jnp.concatenate([w_in, jnp.zeros((d, LANES - ROPE_DIM), F32)], axis=1).astype(BF16)
    cq, ckv, k_rope = mla_in_proj(x, w_in_pad, g_q, g_kv, tabs, seq=seq, tm=512)
    w_uq_h = w_uq.reshape(Q_LORA, MLA_HEADS, NOPE_DIM + ROPE_DIM)
    w_uq_cat = jnp.concatenate([w_uq_h, jnp.zeros((Q_LORA, MLA_HEADS, 2 * LANES - NOPE_DIM - ROPE_DIM), F32)], axis=2)
    q_scale = (NOPE_DIM + ROPE_DIM) ** -0.5 * LOG2_E
    w_uq_cat = (w_uq_cat.reshape(Q_LORA, MLA_HEADS * 2 * LANES) * q_scale).astype(BF16)
    q_cat = mla_q_proj(cq, w_uq_cat, tabs, seq=seq, tm=512, heads_per_step=MLA_HEADS)
    w_kv = jnp.concatenate([w_uk, w_uv], axis=1).astype(BF16)
    kv = matmul(ckv, w_kv, tm=512, tn=1024, out_dtype=BF16)
    o = mla_attention(q_cat, kv, k_rope, batch=batch, seq=seq, tq=512, tk=512, heads=8)
    return proj_residual_ln([o], [w_o.astype(BF16)], x, g, b, w_route, b_route, tm=512)


def kernel(x, ab_w_in, ab_w_pool, ab_pool_scale, ab_rel_bias, ab_w_o, mla_w_in, mla_g_q, mla_g_kv, mla_w_uq, mla_w_uk, mla_w_uv, mla_w_o, ln1_g, ln1_b, moe_w_group, moe_b_group, moe_w_expert, moe_b_expert, moe_w_gate, moe_w_up, moe_w_down, ln2_g, ln2_b):
    batch, seq, d = x.shape
    x = x.reshape(batch * seq, d)
    for i in range(DEPTH):
        j = i // 2
        w_route, b_route = route_weights(moe_w_group[i], moe_b_group[i], moe_w_expert[i], moe_b_expert[i])
        if i % 2 == 0:
            x, x_pk, route = pool_band_layer(x, ab_w_in[j], ab_w_pool[j], ab_pool_scale[j], ab_rel_bias[j], ab_w_o[j],
                                             ln1_g[i], ln1_b[i], w_route, b_route, batch=batch, seq=seq)
        else:
            x, x_pk, route = mla_layer(x, mla_w_in[j], mla_g_q[j], mla_g_kv[j], mla_w_uq[j], mla_w_uk[j], mla_w_uv[j],
                                       mla_w_o[j], ln1_g[i], ln1_b[i], w_route, b_route, batch=batch, seq=seq)
        x = hier_moe_ln(x, x_pk, route, moe_w_gate, moe_w_up, moe_w_down, ln2_g[i], ln2_b[i], layer=i)
    return x.reshape(batch, seq, d)
```
